```python
import math
import jax
import jax.numpy as jnp
from jax import lax
import numpy as np

D_MODEL = 1024
BATCH = 4
SEQ = 4096
DEPTH = 4
DEC_BATCH = 128
DEC_SEQ = 1
PAST_LEN = 2048
PAGE_SIZE = 128

HEAD_DIM = 64
A_HEADS = 8
A_KV_HEADS = 4
MOBA_BLOCK = 256
MOBA_TOPK = 3
C_HEADS = 8
C_KV_HEADS = 4
IDX_HEADS = 8
IDX_DIM = 32
DSA_TOPK = 256
GMLP_CHUNK = 128
GMLP_GROUPS = 8
GMLP_WIDTH = 512
GMLP_GROUP_DIM = GMLP_WIDTH // GMLP_GROUPS
BRANCH_WIDTH = 512
N_GROUPS = 4
EXPERTS_PER_GROUP = 4
N_EXPERTS = N_GROUPS * EXPERTS_PER_GROUP
TOP_K_IN_GROUP = 2
D_FF_EXPERT = 256
ROPE_THETA = 500000.0
ROPE_FRACTION = 4
LN_EPS = 1e-5
DEEPNORM_ALPHA = (2 * DEPTH) ** 0.25
DEEPNORM_BETA = (8 * DEPTH) ** -0.25
QUERY_BLOCK = 32
NEG_INF = -1e30

IN_SPLITS = (A_HEADS * HEAD_DIM, A_KV_HEADS * HEAD_DIM, A_KV_HEADS * HEAD_DIM,
             C_HEADS * HEAD_DIM, C_KV_HEADS * HEAD_DIM, C_KV_HEADS * HEAD_DIM,
             IDX_HEADS * IDX_DIM, IDX_DIM, IDX_HEADS,
             GMLP_WIDTH, GMLP_WIDTH,
             D_MODEL, D_MODEL, D_MODEL)
N_IN = sum(IN_SPLITS)

kernel_name = 'hybrid_moba_gmlp_dsa_hmoe_step'


def split_columns(h):
    parts = []
    off = 0
    for w in IN_SPLITS:
        parts.append(h[..., off:off + w])
        off += w
    return parts


def layer_norm(x, g, b):
    xf = x.astype(jnp.float32)
    mu = jnp.mean(xf, axis=-1, keepdims=True)
    var = jnp.mean(jnp.square(xf - mu), axis=-1, keepdims=True)
    return ((xf - mu) * lax.rsqrt(var + LN_EPS) * g + b).astype(x.dtype)


def partial_rope(x, pos):
    rot = x.shape[-1] // ROPE_FRACTION
    half = rot // 2
    inv_freq = ROPE_THETA ** (-jnp.arange(half, dtype=jnp.float32) / half)
    ang = pos.astype(jnp.float32)[:, None] * inv_freq[None, :]
    cos = jnp.cos(ang)[:, None, :].astype(x.dtype)
    sin = jnp.sin(ang)[:, None, :].astype(x.dtype)
    x1 = x[..., :half]
    x2 = x[..., half:rot]
    return jnp.concatenate([x1 * cos - x2 * sin, x2 * cos + x1 * sin, x[..., rot:]], axis=-1)


def moba_attention(q, k, v, q_start):
    b, t, h, dh = q.shape
    L, hkv = k.shape[1], k.shape[2]
    grp = h // hkv
    nb = -(-L // MOBA_BLOCK)
    pad = nb * MOBA_BLOCK - L
    kb = jnp.pad(k, ((0, 0), (0, pad), (0, 0), (0, 0))).reshape(b, nb, MOBA_BLOCK, hkv, dh)
    vb = jnp.pad(v, ((0, 0), (0, pad), (0, 0), (0, 0))).reshape(b, nb, MOBA_BLOCK, hkv, dh)
    k_mean = jnp.mean(kb.astype(jnp.float32), axis=2)
    kbt = kb.transpose(0, 3, 1, 2, 4)
    vbt = vb.transpose(0, 3, 1, 2, 4)
    n_sel = min(MOBA_TOPK, nb)
    qb = math.gcd(t, QUERY_BLOCK)
    nq = t // qb
    scale = dh ** -0.5
    q_blocks = q.reshape(b, nq, qb, hkv, grp, dh).transpose(1, 0, 2, 3, 4, 5)
    pos_blocks = (q_start + jnp.arange(t, dtype=jnp.int32)).reshape(nq, qb)
    b_idx = jnp.arange(b)[:, None, None, None, None]
    h_idx = jnp.arange(hkv)[None, None, :, None, None]
    blk_ids = jnp.arange(nb)
    offs = jnp.arange(MOBA_BLOCK, dtype=jnp.int32)

    def one_block(args):
        qq, pos = args
        j = pos // MOBA_BLOCK
        gate = jnp.einsum('bqkgd,bnkd->bqkgn', qq.astype(jnp.float32), k_mean)
        cand = blk_ids[None, :] < j[:, None]
        gate = jnp.where(cand[None, :, None, None, :], gate, NEG_INF)
        _, top = lax.top_k(gate, n_sel)
        own = jnp.broadcast_to(j[None, :, None, None, None], top.shape[:-1] + (1,)).astype(top.dtype)
        sel = jnp.concatenate([top, own], axis=-1)
        kg = kbt[b_idx, h_idx, sel]
        vg = vbt[b_idx, h_idx, sel]
        s = jnp.einsum('bqkgd,bqkgnsd->bqkgns', qq, kg).astype(jnp.float32) * scale
        slot_ok = jnp.concatenate([jnp.arange(n_sel)[None, :] < j[:, None],
                                   jnp.ones((qb, 1), dtype=bool)], axis=1)
        key_pos = sel[..., None] * MOBA_BLOCK + offs
        ok = slot_ok[None, :, None, None, :, None] & (key_pos <= pos[None, :, None, None, None, None])
        s = jnp.where(ok, s, NEG_INF)
        shp = s.shape
        p = jax.nn.softmax(s.reshape(shp[:4] + (-1,)), axis=-1).reshape(shp)
        return jnp.einsum('bqkgns,bqkgnsd->bqkgd', p.astype(vg.dtype), vg)

    out = lax.map(one_block, (q_blocks, pos_blocks))
    return out.transpose(1, 0, 2, 3, 4, 5).reshape(b, t, h * dh)


def dsa_attention(q, k, v, iq, ik, iw, q_start):
    b, t, h, dh = q.shape
    L, hkv = k.shape[1], k.shape[2]
    grp = h // hkv
    n_top = min(DSA_TOPK, L // 4)
    qb = math.gcd(t, QUERY_BLOCK)
    nq = t // qb
    scale = dh ** -0.5
    q_blocks = q.reshape(b, nq, qb, hkv, grp, dh).transpose(1, 0, 2, 3, 4, 5)
    iq_blocks = iq.reshape(b, nq, qb, IDX_HEADS, IDX_DIM).transpose(1, 0, 2, 3, 4)
    iw_blocks = iw.reshape(b, nq, qb, IDX_HEADS).transpose(1, 0, 2, 3)
    pos_blocks = (q_start + jnp.arange(t, dtype=jnp.int32)).reshape(nq, qb)
    key_ids = jnp.arange(L, dtype=jnp.int32)
    b_idx = jnp.arange(b)[:, None, None]

    def one_block(args):
        qq, iqq, iww, pos = args
        rel = jax.nn.relu(jnp.einsum('bqhd,bsd->bqhs', iqq, ik).astype(jnp.float32) * IDX_DIM ** -0.5)
        score = jnp.einsum('bqh,bqhs->bqs', iww.astype(jnp.float32) * IDX_HEADS ** -0.5, rel)
        score = jnp.where(key_ids[None, None, :] <= pos[None, :, None], score, NEG_INF)
        vals, idx = lax.top_k(score, n_top)
        valid = vals > 0.5 * NEG_INF
        kg = k[b_idx, idx]
        vg = v[b_idx, idx]
        s = jnp.einsum('bqkgd,bqnkd->bqkgn', qq, kg).astype(jnp.float32) * scale
        s = jnp.where(valid[:, :, None, None, :], s, NEG_INF)
        p = jax.nn.softmax(s, axis=-1)
        return jnp.einsum('bqkgn,bqnkd->bqkgd', p.astype(vg.dtype), vg)

    out = lax.map(one_block, (q_blocks, iq_blocks, iw_blocks, pos_blocks))
    return out.transpose(1, 0, 2, 3, 4, 5).reshape(b, t, h * dh)


def gmlp_spatial(u, v, w_s, b_s):
    b, t, _ = v.shape
    nc = -(-t // GMLP_CHUNK)
    pad = nc * GMLP_CHUNK - t
    vc = jnp.pad(v, ((0, 0), (0, pad), (0, 0))).reshape(b, nc, GMLP_CHUNK, GMLP_GROUPS, GMLP_GROUP_DIM)
    causal = jnp.tril(jnp.ones((GMLP_CHUNK, GMLP_CHUNK), dtype=bool))
    ws = jnp.where(causal[None], w_s, 0)
    s = jnp.einsum('gts,bcsgd->bctgd', ws, vc) + b_s.T[None, None, :, :, None]
    s = s.reshape(b, nc * GMLP_CHUNK, GMLP_WIDTH)[:, :t]
    return u * s


def hier_moe(x, w_rg, b_rg, w_re, b_re, w_e1, w_e3, w_e2):
    shp = x.shape
    xf = x.reshape(-1, shp[-1])
    n = xf.shape[0]
    rows = jnp.arange(n)
    g_logits = (xf @ w_rg).astype(jnp.float32) + b_rg
    g_prob = jax.nn.softmax(g_logits, axis=-1)
    g_sel = jnp.argmax(g_logits, axis=-1)
    g_w = g_prob[rows, g_sel][:, None]
    e_logits = ((xf @ w_re).astype(jnp.float32) + b_re).reshape(n, N_GROUPS, EXPERTS_PER_GROUP)
    e_in = e_logits[rows, g_sel]
    top_v, top_i = lax.top_k(e_in, TOP_K_IN_GROUP)
    w_sel = jax.nn.softmax(top_v, axis=-1) * g_w
    e_ids = g_sel[:, None] * EXPERTS_PER_GROUP + top_i
    comb = jnp.sum(jax.nn.one_hot(e_ids, N_EXPERTS, dtype=jnp.float32) * w_sel[..., None], axis=1)
    hg = jnp.einsum('nd,edf->nef', xf, w_e1)
    hu = jnp.einsum('nd,edf->nef', xf, w_e3)
    hh = jax.nn.silu(hg) * hu * comb[:, :, None].astype(x.dtype)
    return jnp.einsum('nef,efd->nd', hh, w_e2).reshape(shp)


def trunk_layer(x, q_start, past, p):
    b, t, _ = x.shape
    pos = q_start + jnp.arange(t, dtype=jnp.int32)
    h = x @ p['w_in']
    aq, ak, av, cq, ck, cv, iq, ik, iw, gu, gv, ga, gb, gc = split_columns(h)
    aq = partial_rope(aq.reshape(b, t, A_HEADS, HEAD_DIM), pos)
    ak = partial_rope(ak.reshape(b, t, A_KV_HEADS, HEAD_DIM), pos)
    av = av.reshape(b, t, A_KV_HEADS, HEAD_DIM)
    cq = partial_rope(cq.reshape(b, t, C_HEADS, HEAD_DIM), pos)
    ck = partial_rope(ck.reshape(b, t, C_KV_HEADS, HEAD_DIM), pos)
    cv = cv.reshape(b, t, C_KV_HEADS, HEAD_DIM)
    iq = partial_rope(iq.reshape(b, t, IDX_HEADS, IDX_DIM), pos)
    ik = partial_rope(ik[:, :, None, :], pos)[:, :, 0]
    new_a_kv = jnp.stack([ak, av], axis=2)
    new_c_kv = jnp.stack([ck, cv], axis=2)
    if past is None:
        a_kv, c_kv, c_kidx = new_a_kv, new_c_kv, ik
    else:
        a_kv = jnp.concatenate([past[0], new_a_kv], axis=1)
        c_kv = jnp.concatenate([past[1], new_c_kv], axis=1)
        c_kidx = jnp.concatenate([past[2], ik], axis=1)
    o_a = moba_attention(aq, a_kv[:, :, 0], a_kv[:, :, 1], q_start)
    o_c = dsa_attention(cq, c_kv[:, :, 0], c_kv[:, :, 1], iq, c_kidx, iw, q_start)
    u = jax.nn.gelu(gu)
    vv = layer_norm(jax.nn.gelu(gv), p['ln_v_g'], p['ln_v_b'])
    o_b = gmlp_spatial(u, vv, p['w_s'], p['b_s'])
    merged = (jax.nn.sigmoid(ga) * (o_a @ p['w_pa'])
              + jax.nn.sigmoid(gb) * (o_b @ p['w_pb'])
              + jax.nn.sigmoid(gc) * (o_c @ p['w_pc']))
    x = layer_norm(DEEPNORM_ALPHA * x + merged @ p['w_out'], p['ln1_g'], p['ln1_b'])
    ffn = hier_moe(x, p['w_rg'], p['b_rg'], p['w_re'], p['b_re'], p['w_e1'], p['w_e3'], p['w_e2'])
    x = layer_norm(DEEPNORM_ALPHA * x + ffn, p['ln2_g'], p['ln2_b'])
    return x, new_a_kv, new_c_kv, ik, vv


def setup_inputs(seed: int = 0) -> dict:
    key = jax.random.key(seed)
    ks = jax.random.split(key, 32)
    f32 = jnp.float32
    n_pages = PAST_LEN // PAGE_SIZE
    n_phys = (DEC_BATCH * n_pages * 5) // 4

    def nrm(k, shape, scale):
        return jax.random.normal(k, shape, f32) * scale

    x_prompt = nrm(ks[0], (BATCH, SEQ, D_MODEL), 1.0)
    x_sample = nrm(ks[1], (DEC_BATCH, DEC_SEQ, D_MODEL), 1.0)
    cache_a_kv = nrm(ks[2], (DEPTH, n_phys, PAGE_SIZE, 2, A_KV_HEADS, HEAD_DIM), 1.0)
    cache_c_kv = nrm(ks[3], (DEPTH, n_phys, PAGE_SIZE, 2, C_KV_HEADS, HEAD_DIM), 1.0)
    cache_c_kidx = nrm(ks[4], (DEPTH, n_phys, PAGE_SIZE, IDX_DIM), 1.0)
    page_table = jax.random.permutation(ks[5], n_phys)[:DEC_BATCH * n_pages].reshape(DEC_BATCH, n_pages).astype(jnp.int32)
    w_in = nrm(ks[6], (DEPTH, D_MODEL, N_IN), D_MODEL ** -0.5)
    w_pa = nrm(ks[7], (DEPTH, BRANCH_WIDTH, D_MODEL), BRANCH_WIDTH ** -0.5)
    w_pb = nrm(ks[8], (DEPTH, BRANCH_WIDTH, D_MODEL), BRANCH_WIDTH ** -0.5)
    w_pc = nrm(ks[9], (DEPTH, BRANCH_WIDTH, D_MODEL), BRANCH_WIDTH ** -0.5)
    w_out = nrm(ks[10], (DEPTH, D_MODEL, D_MODEL), D_MODEL ** -0.5 * DEEPNORM_BETA)
    w_s = nrm(ks[11], (DEPTH, GMLP_GROUPS, GMLP_CHUNK, GMLP_CHUNK), GMLP_CHUNK ** -0.5)
    b_s = 1.0 + nrm(ks[12], (DEPTH, GMLP_GROUPS, GMLP_CHUNK), 0.02)
    ln_v_g = 1.0 + nrm(ks[13], (DEPTH, GMLP_WIDTH), 0.02)
    ln_v_b = nrm(ks[14], (DEPTH, GMLP_WIDTH), 0.02)
    ln1_g = 1.0 + nrm(ks[15], (DEPTH, D_MODEL), 0.02)
    ln1_b = nrm(ks[16], (DEPTH, D_MODEL), 0.02)
    ln2_g = 1.0 + nrm(ks[17], (DEPTH, D_MODEL), 0.02)
    ln2_b = nrm(ks[18], (DEPTH, D_MODEL), 0.02)
    w_rg = nrm(ks[19], (DEPTH, D_MODEL, N_GROUPS), D_MODEL ** -0.5)
    b_rg = nrm(ks[20], (DEPTH, N_GROUPS), 0.01)
    w_re = nrm(ks[21], (DEPTH, D_MODEL, N_EXPERTS), D_MODEL ** -0.5)
    b_re = nrm(ks[22], (DEPTH, N_EXPERTS), 0.01)
    w_e1 = nrm(ks[23], (DEPTH, N_EXPERTS, D_MODEL, D_FF_EXPERT), D_MODEL ** -0.5)
    w_e3 = nrm(ks[24], (DEPTH, N_EXPERTS, D_MODEL, D_FF_EXPERT), D_MODEL ** -0.5)
    w_e2 = nrm(ks[25], (DEPTH, N_EXPERTS, D_FF_EXPERT, D_MODEL), D_FF_EXPERT ** -0.5 * DEEPNORM_BETA)
    return {'x_prompt': x_prompt, 'x_sample': x_sample,
            'cache_a_kv': cache_a_kv, 'cache_c_kv': cache_c_kv, 'cache_c_kidx': cache_c_kidx,
            'page_table': page_table,
            'w_in': w_in, 'w_pa': w_pa, 'w_pb': w_pb, 'w_pc': w_pc, 'w_out': w_out,
            'w_s': w_s, 'b_s': b_s, 'ln_v_g': ln_v_g, 'ln_v_b': ln_v_b,
            'ln1_g': ln1_g, 'ln1_b': ln1_b, 'ln2_g': ln2_g, 'ln2_b': ln2_b,
            'w_rg': w_rg, 'b_rg': b_rg, 'w_re': w_re, 'b_re': b_re,
            'w_e1': w_e1, 'w_e3': w_e3, 'w_e2': w_e2}


def reference(x_prompt, x_sample, cache_a_kv, cache_c_kv, cache_c_kidx, page_table,
              w_in, w_pa, w_pb, w_pc, w_out, w_s, b_s, ln_v_g, ln_v_b,
              ln1_g, ln1_b, ln2_g, ln2_b, w_rg, b_rg, w_re, b_re, w_e1, w_e3, w_e2):
    n_pages = PAST_LEN // PAGE_SIZE
    db = x_sample.shape[0]
    bp, tp = x_prompt.shape[0], x_prompt.shape[1]
    xp, xs = x_prompt, x_sample
    a_kv_p, c_kv_p, kidx_p = [], [], []
    a_kv_s, c_kv_s, kidx_s, bv_s = [], [], [], []
    for l in range(DEPTH):
        p = {'w_in': w_in[l], 'w_pa': w_pa[l], 'w_pb': w_pb[l], 'w_pc': w_pc[l], 'w_out': w_out[l],
             'w_s': w_s[l], 'b_s': b_s[l], 'ln_v_g': ln_v_g[l], 'ln_v_b': ln_v_b[l],
             'ln1_g': ln1_g[l], 'ln1_b': ln1_b[l], 'ln2_g': ln2_g[l], 'ln2_b': ln2_b[l],
             'w_rg': w_rg[l], 'b_rg': b_rg[l], 'w_re': w_re[l], 'b_re': b_re[l],
             'w_e1': w_e1[l], 'w_e3': w_e3[l], 'w_e2': w_e2[l]}
        xp, akv, ckv, kidx, _ = trunk_layer(xp, 0, None, p)
        a_kv_p.append(akv.reshape(bp, tp // PAGE_SIZE, PAGE_SIZE, 2, A_KV_HEADS, HEAD_DIM))
        c_kv_p.append(ckv.reshape(bp, tp // PAGE_SIZE, PAGE_SIZE, 2, C_KV_HEADS, HEAD_DIM))
        kidx_p.append(kidx.reshape(bp, tp // PAGE_SIZE, PAGE_SIZE, IDX_DIM))
        past = (cache_a_kv[l][page_table].reshape(db, n_pages * PAGE_SIZE, 2, A_KV_HEADS, HEAD_DIM),
                cache_c_kv[l][page_table].reshape(db, n_pages * PAGE_SIZE, 2, C_KV_HEADS, HEAD_DIM),
                cache_c_kidx[l][page_table].reshape(db, n_pages * PAGE_SIZE, IDX_DIM))
        xs, akv, ckv, kidx, bv = trunk_layer(xs, PAST_LEN, past, p)
        a_kv_s.append(akv)
        c_kv_s.append(ckv)
        kidx_s.append(kidx)
        bv_s.append(bv)
    return (xp, xs,
            jnp.stack(a_kv_p), jnp.stack(c_kv_p), jnp.stack(kidx_p),
            jnp.stack(a_kv_s), jnp.stack(c_kv_s), jnp.stack(kidx_s), jnp.stack(bv_s))
```

```python
import functools
import math

import jax
import jax.numpy as jnp
from jax import lax
from jax.experimental import pallas as pl
from jax.experimental.pallas import tpu as pltpu

HEAD_DIM = 64
N_HEADS = 8
N_KV_HEADS = 4
GROUP = N_HEADS // N_KV_HEADS
MOBA_BLOCK = 256
MOBA_TOPK = 3
IDX_HEADS = 8
IDX_DIM = 32
DSA_TOPK = 256
GMLP_CHUNK = 128
GMLP_GROUPS = 8
GMLP_WIDTH = 512
BRANCH_WIDTH = 512
N_GROUPS = 4
EXPERTS_PER_GROUP = 4
N_EXPERTS = N_GROUPS * EXPERTS_PER_GROUP
ROPE_THETA = 500000.0
ROPE_FRACTION = 4
LN_EPS = 1e-5
NEG_INF = -1e30
DEEPNORM_ALPHA = (2 * 4) ** 0.25

LANES = 128
MXU_DTYPE = jnp.bfloat16
VMEM_LIMIT = 56 * 1024 * 1024

QW = N_HEADS * HEAD_DIM
KW = N_KV_HEADS * HEAD_DIM
IQW = IDX_HEADS * IDX_DIM
INT_MIN = -2147483648
HALF_NEG_KEY = -1892283083

f32 = jnp.float32


def _nt_dot(a, b):
    return lax.dot_general(a, b, (((1,), (1,)), ((), ())), preferred_element_type=f32)


def _dot(a, b):
    return jnp.dot(a, b, preferred_element_type=f32)


def _layer_norm(y, g, b):
    mu = jnp.mean(y, axis=-1, keepdims=True)
    d = y - mu
    var = jnp.mean(d * d, axis=-1, keepdims=True)
    return d * lax.rsqrt(var + LN_EPS) * g + b


def _gelu_tanh(x):
    cdf = 0.5 * (1.0 + jnp.tanh(math.sqrt(2.0 / math.pi) * (x + 0.044715 * (x * x * x))))
    return x * cdf


def _sigmoid(x):
    return 1.0 / (1.0 + jnp.exp(-x))


def _sortable(x):
    bits = pltpu.bitcast(x, jnp.int32)
    return jnp.where(bits < 0, bits ^ jnp.int32(0x7FFFFFFF), bits)


def _const_spec(shape):
    nd = len(shape)
    return pl.BlockSpec(shape, lambda *_: (0,) * nd, pipeline_mode=pl.Buffered(1))


def _rope_chunk(h, c, sa, sb, half):
    return h * c + pltpu.roll(h, half, 1) * sa + pltpu.roll(h, LANES - half, 1) * sb


def _proj_body(x_ref, wm_ref, wg_ref, t64_ref, t32_ref, tl_ref, lng_ref, lnb_ref, ws_ref, bs_ref,
               qa_ref, qc_ref, kva_ref, kvc_ref, kvab_ref, kvcb_ref, iq_ref, ikw_ref, ikr_ref, ob_ref, vv_ref,
               *, chunked):
    xb = x_ref[...].astype(MXU_DTYPE)
    tm = xb.shape[0]
    c64, sa64, sb64 = t64_ref[0], t64_ref[1], t64_ref[2]
    c32, sa32, sb32 = t32_ref[0], t32_ref[1], t32_ref[2]
    cl, sal, sbl = tl_ref[0], tl_ref[1], tl_ref[2]
    h64 = (HEAD_DIM // ROPE_FRACTION) // 2
    h32 = (IDX_DIM // ROPE_FRACTION) // 2
    scale = HEAD_DIM ** -0.5

    hq = _dot(xb, wm_ref[:, 0:2 * QW])
    for c in range(2 * QW // LANES):
        r = _rope_chunk(hq[:, c * LANES:(c + 1) * LANES], c64, sa64, sb64, h64) * scale
        dst = qa_ref if c < QW // LANES else qc_ref
        cc = c % (QW // LANES)
        dst[:, cc * LANES:(cc + 1) * LANES] = r.astype(dst.dtype)

    o = 2 * QW
    hk = _dot(xb, wm_ref[:, o:o + 2 * KW])
    hv = _dot(xb, wm_ref[:, o + 2 * KW:o + 4 * KW])
    for c in range(2 * KW // LANES):
        r = _rope_chunk(hk[:, c * LANES:(c + 1) * LANES], c64, sa64, sb64, h64)
        v = hv[:, c * LANES:(c + 1) * LANES]
        dst, dstb = (kva_ref, kvab_ref) if c < KW // LANES else (kvc_ref, kvcb_ref)
        cc = c % (KW // LANES)
        dst[:, cc * LANES:(cc + 1) * LANES] = r
        dst[:, KW + cc * LANES:KW + (cc + 1) * LANES] = v
        dstb[:, cc * LANES:(cc + 1) * LANES] = r.astype(dstb.dtype)
        dstb[:, KW + cc * LANES:KW + (cc + 1) * LANES] = v.astype(dstb.dtype)

    o = 2 * QW + 4 * KW
    hi = _dot(xb, wm_ref[:, o:o + IQW + LANES])
    for c in range(IQW // LANES):
        r = _rope_chunk(hi[:, c * LANES:(c + 1) * LANES], c32, sa32, sb32, h32)
        iq_ref[:, c * LANES:(c + 1) * LANES] = r.astype(iq_ref.dtype)
    last = _rope_chunk(hi[:, IQW:IQW + LANES], cl, sal, sbl, h32)
    ikw_ref[...] = last
    lane = lax.broadcasted_iota(jnp.int32, (tm, LANES), 1)
    t = jnp.where(lane < IDX_DIM, last, 0.0)
    t = t + pltpu.roll(t, IDX_DIM, 1)
    t = t + pltpu.roll(t, 2 * IDX_DIM, 1)
    for c in range(IQW // LANES):
        ikr_ref[:, c * LANES:(c + 1) * LANES] = t.astype(ikr_ref.dtype)

    hg = _dot(xb, wg_ref[...])
    u = _gelu_tanh(hg[:, 0:GMLP_WIDTH])
    vv = _layer_norm(_gelu_tanh(hg[:, GMLP_WIDTH:2 * GMLP_WIDTH]), lng_ref[...], lnb_ref[...])
    vv_ref[...] = vv
    if chunked:
        vb = vv.astype(MXU_DTYPE)
        gd = GMLP_WIDTH // GMLP_GROUPS
        lane_c = lax.broadcasted_iota(jnp.int32, (GMLP_CHUNK, LANES), 1)
        for c in range(tm // GMLP_CHUNK):
            rows = slice(c * GMLP_CHUNK, (c + 1) * GMLP_CHUNK)
            for pr in range(GMLP_WIDTH // LANES):
                vp = vb[rows, pr * LANES:(pr + 1) * LANES]
                g0 = (pr * LANES) // gd
                r0 = _dot(ws_ref[g0], vp)
                r1 = _dot(ws_ref[g0 + 1], vp)
                s = jnp.where(lane_c < gd, r0, r1) + bs_ref[:, pr * LANES:(pr + 1) * LANES]
                ob_ref[rows, pr * LANES:(pr + 1) * LANES] = (u[rows, pr * LANES:(pr + 1) * LANES] * s).astype(ob_ref.dtype)
    else:
        ob_ref[...] = (u * (vv * ws_ref[...] + bs_ref[...])).astype(ob_ref.dtype)


def _proj_call(x, wm, wg, t64, t32, tl, lng, lnb, ws, bs, *, tm, chunked):
    n, d = x.shape
    nt = t64.shape[1] // tm
    grid = (n // tm,)
    row = lambda w: pl.BlockSpec((tm, w), lambda i: (i, 0))
    tab = pl.BlockSpec((3, tm, LANES), lambda i: (0, i % nt, 0))
    outs = [
        (QW, MXU_DTYPE), (QW, MXU_DTYPE),
        (2 * KW, f32), (2 * KW, f32),
        (2 * KW, MXU_DTYPE), (2 * KW, MXU_DTYPE),
        (IQW, MXU_DTYPE),
        (LANES, f32),
        (IQW, MXU_DTYPE),
        (GMLP_WIDTH, MXU_DTYPE),
        (GMLP_WIDTH, f32),
    ]
    return pl.pallas_call(
        functools.partial(_proj_body, chunked=chunked),
        grid=grid,
        in_specs=[row(d), _const_spec(wm.shape), _const_spec(wg.shape), tab, tab, tab,
                  _const_spec(lng.shape), _const_spec(lnb.shape), _const_spec(ws.shape), _const_spec(bs.shape)],
        out_specs=[row(w) for w, _ in outs],
        out_shape=[jax.ShapeDtypeStruct((n, w), dt) for w, dt in outs],
        compiler_params=pltpu.CompilerParams(dimension_semantics=("parallel",), vmem_limit_bytes=VMEM_LIMIT),
        name="proj",
    )(x, wm, wg, t64, t32, tl, lng, lnb, ws, bs)


def _head_rows(q_ref, kvh, tq):
    qp = q_ref[:, kvh * LANES:(kvh + 1) * LANES].astype(f32)
    rl = pltpu.roll(qp, HEAD_DIM, 1)
    lo = lax.broadcasted_iota(jnp.int32, (tq, LANES), 1) < HEAD_DIM
    if kvh % 2 == 0:
        g0, g1 = jnp.where(lo, qp, 0.0), jnp.where(lo, rl, 0.0)
    else:
        g0, g1 = jnp.where(lo, 0.0, rl), jnp.where(lo, 0.0, qp)
    return jnp.concatenate([g0, g1], axis=0).astype(MXU_DTYPE)


def _head_out(acc, l, kvh, tq):
    o = acc / l
    o0, o1 = o[0:tq], o[tq:2 * tq]
    lo = lax.broadcasted_iota(jnp.int32, (tq, LANES), 1) < HEAD_DIM
    if kvh % 2 == 0:
        return jnp.where(lo, o0, pltpu.roll(o1, HEAD_DIM, 1))
    return jnp.where(lo, pltpu.roll(o0, HEAD_DIM, 1), o1)


def _softmax_step(s, mask, m, l, acc, v_t):
    s = jnp.where(mask, s, NEG_INF)
    m_new = jnp.maximum(m, jnp.max(s, axis=1, keepdims=True))
    p = jnp.where(mask, jnp.exp(s - m_new), 0.0)
    alpha = jnp.exp(m - m_new)
    l_new = alpha * l + jnp.sum(p, axis=1, keepdims=True)
    acc_new = alpha * acc + _dot(p.astype(MXU_DTYPE), v_t)
    return m_new, l_new, acc_new


def _moba_body(q_ref, kv_ref, o_ref, kmean_sc, acc_sc, *, tq):
    i = pl.program_id(1)
    t = kv_ref.shape[0]
    nb = t // MOBA_BLOCK
    rows = GROUP * tq

    @pl.when(i == 0)
    def _():
        kmean_sc[...] = jnp.zeros_like(kmean_sc)
        for n in range(nb):
            blk = kv_ref[n * MOBA_BLOCK:(n + 1) * MOBA_BLOCK, 0:KW].astype(f32)
            kmean_sc[n:n + 1, :] = jnp.mean(blk, axis=0, keepdims=True)

    lane = lax.broadcasted_iota(jnp.int32, (rows, LANES), 1)
    key_in_blk = lax.broadcasted_iota(jnp.int32, (rows, MOBA_BLOCK), 1)
    row_in_blk = lax.broadcasted_iota(jnp.int32, (rows, MOBA_BLOCK), 0) % tq
    causal = key_in_blk <= row_in_blk

    for kvh in range(N_KV_HEADS):
        pc = kvh // 2
        ks = slice(pc * LANES, (pc + 1) * LANES)
        vs = slice(KW + pc * LANES, KW + (pc + 1) * LANES)
        qh = _head_rows(q_ref, kvh, tq)

        gate = _nt_dot(qh, kmean_sc[:, ks].astype(MXU_DTYPE))
        g = jnp.where(lane < i, gate, NEG_INF)
        sel = jnp.zeros((rows, LANES), f32)
        for _ in range(MOBA_TOPK):
            mx = jnp.max(g, axis=1, keepdims=True)
            first = jnp.min(jnp.where(g == mx, lane, 1 << 20), axis=1, keepdims=True)
            hit = lane == first
            sel = jnp.where(hit & (mx > 0.5 * NEG_INF), 1.0, sel)
            g = jnp.where(hit, -3e38, g)

        own = pl.multiple_of(i * MOBA_BLOCK, MOBA_BLOCK)
        s = _nt_dot(qh, kv_ref[pl.ds(own, MOBA_BLOCK), ks])
        m0 = jnp.full((rows, 1), NEG_INF, f32)
        l0 = jnp.zeros((rows, 1), f32)
        m, l, acc = _softmax_step(s, causal, m0, l0, jnp.zeros((rows, LANES), f32),
                                  kv_ref[pl.ds(own, MOBA_BLOCK), vs])
        acc_sc[...] = acc

        def past(n, carry):
            m, l = carry
            start = pl.multiple_of(n * MOBA_BLOCK, MOBA_BLOCK)
            s = _nt_dot(qh, kv_ref[pl.ds(start, MOBA_BLOCK), ks])
            chosen = jnp.max(jnp.where(lane == n, sel, 0.0), axis=1, keepdims=True) > 0.5
            mask = jnp.broadcast_to(chosen, s.shape)
            m, l, acc = _softmax_step(s, mask, m, l, acc_sc[...], kv_ref[pl.ds(start, MOBA_BLOCK), vs])
            acc_sc[...] = acc
            return m, l

        m, l = lax.fori_loop(0, i, past, (m, l))
        o_ref[:, kvh * LANES:(kvh + 1) * LANES] = _head_out(acc_sc[...], l, kvh, tq).astype(o_ref.dtype)


def _moba_call(q, kv, *, batch, t):
    tq = MOBA_BLOCK
    nq = t // tq
    return pl.pallas_call(
        functools.partial(_moba_body, tq=tq),
        grid=(batch, nq),
        in_specs=[pl.BlockSpec((tq, QW), lambda b, i: (b * nq + i, 0)),
                  pl.BlockSpec((t, 2 * KW), lambda b, i: (b, 0))],
        out_specs=pl.BlockSpec((tq, QW), lambda b, i: (b * nq + i, 0)),
        out_shape=jax.ShapeDtypeStruct(q.shape, MXU_DTYPE),
        scratch_shapes=[pltpu.VMEM((LANES, KW), f32), pltpu.VMEM((GROUP * tq, LANES), f32)],
        compiler_params=pltpu.CompilerParams(dimension_semantics=("parallel", "arbitrary"),
                                             vmem_limit_bytes=VMEM_LIMIT),
        name="moba",
    )(q, kv)


def _select_threshold(count_ge, n_top, rows):
    def step(it, t_u):
        bit = lax.shift_left(jnp.int32(1), jnp.int32(31) - it)
        cand_u = t_u | bit
        cnt = count_ge(cand_u ^ jnp.int32(INT_MIN))
        return jnp.where(cnt >= n_top, cand_u, t_u)
    t_u = lax.fori_loop(0, 32, step, jnp.zeros((rows, 1), jnp.int32))
    return t_u ^ jnp.int32(INT_MIN)


def _dsa_body(iq_ref, ikw_ref, ikr_ref, q_ref, kv_ref, o_ref, key_sc, cut_sc, acc_sc, *, tq, n_top, idx_bits):
    i = pl.program_id(1)
    ntile = i + 1
    rows = GROUP * tq
    lane_q = lax.broadcasted_iota(jnp.int32, (tq, IQW), 1)
    col = lax.broadcasted_iota(jnp.int32, (tq, tq), 1)
    rowi = lax.broadcasted_iota(jnp.int32, (tq, tq), 0)
    half_neg = jnp.int32(HALF_NEG_KEY)

    iqv = iq_ref[...]
    iqm = [jnp.where((lane_q >= h * IDX_DIM) & (lane_q < (h + 1) * IDX_DIM), iqv, jnp.zeros_like(iqv))
           for h in range(IDX_HEADS)]
    iw = [ikw_ref[:, IDX_DIM + h:IDX_DIM + h + 1] for h in range(IDX_HEADS)]
    sc_scale = (IDX_DIM ** -0.5) * (IDX_HEADS ** -0.5)

    def score_tile(kt, _):
        ik_t = ikr_ref[pl.ds(pl.multiple_of(kt * tq, tq), tq), :]
        sc = jnp.zeros((tq, tq), f32)
        for h in range(IDX_HEADS):
            sc = sc + iw[h] * jnp.maximum(_nt_dot(iqm[h], ik_t), 0.0)
        sc = sc * sc_scale
        sc = jnp.where(kt * tq + col <= i * tq + rowi, sc, NEG_INF)
        key_sc[kt] = _sortable(sc)
        return 0

    lax.fori_loop(0, ntile, score_tile, 0)

    def count_where(pred):
        def body(kt, c):
            return c + jnp.where(pred(key_sc[kt], kt), 1, 0)
        c = lax.fori_loop(0, ntile, body, jnp.zeros((tq, tq), jnp.int32))
        return jnp.sum(c, axis=1, keepdims=True)

    thr = _select_threshold(lambda cand: count_where(lambda k, kt: k >= cand), n_top, tq)
    cnt_gt = count_where(lambda k, kt: k > thr)
    cnt_eq = count_where(lambda k, kt: k == thr)
    need = n_top - cnt_gt

    cut_sc[...] = jnp.full(cut_sc.shape, 1 << 30, jnp.int32)
    excess = (cnt_eq > need) & (thr > half_neg)

    @pl.when(jnp.max(jnp.where(excess, 1, 0)) > 0)
    def _():
        def step(it, t_i):
            cand = t_i | lax.shift_left(jnp.int32(1), jnp.int32(idx_bits - 1) - it)
            cnt = count_where(lambda k, kt: (k == thr) & (kt * tq + col < cand))
            return jnp.where(cnt < need, cand, t_i)
        t_i = lax.fori_loop(0, idx_bits, step, jnp.zeros((tq, 1), jnp.int32))
        cut_sc[...] = jnp.broadcast_to(jnp.where(excess, t_i, 1 << 30), cut_sc.shape)

    cut = cut_sc[:, 0:1]

    qh = [_head_rows(q_ref, kvh, tq) for kvh in range(N_KV_HEADS)]
    acc_sc[...] = jnp.zeros_like(acc_sc)

    def attend(kt, carry):
        start = pl.multiple_of(kt * tq, tq)
        k = key_sc[kt]
        keep = ((k > thr) | ((k == thr) & (kt * tq + col <= cut))) & (k > half_neg)
        mask = jnp.concatenate([keep] * GROUP, axis=0)
        out = []
        for kvh in range(N_KV_HEADS):
            pc = kvh // 2
            m, l = carry[2 * kvh], carry[2 * kvh + 1]
            s = _nt_dot(qh[kvh], kv_ref[pl.ds(start, tq), pc * LANES:(pc + 1) * LANES])
            m, l, acc = _softmax_step(s, mask, m, l, acc_sc[kvh],
                                      kv_ref[pl.ds(start, tq), KW + pc * LANES:KW + (pc + 1) * LANES])
            acc_sc[kvh] = acc
            out += [m, l]
        return tuple(out)

    init = tuple(x for _ in range(N_KV_HEADS)
                 for x in (jnp.full((rows, 1), NEG_INF, f32), jnp.zeros((rows, 1), f32)))
    fin = lax.fori_loop(0, ntile, attend, init)
    for kvh in range(N_KV_HEADS):
        o_ref[:, kvh * LANES:(kvh + 1) * LANES] = _head_out(acc_sc[kvh], fin[2 * kvh + 1], kvh, tq).astype(o_ref.dtype)


def _dsa_call(iq, ikw, ikr, q, kv, *, batch, t):
    tq = 256
    nq = t // tq
    n_top = min(DSA_TOPK, t // 4)
    assert n_top <= tq and t % tq == 0
    rowspec = lambda w: pl.BlockSpec((tq, w), lambda b, i: (b * nq + i, 0))
    seqspec = lambda w: pl.BlockSpec((t, w), lambda b, i: (b, 0))
    return pl.pallas_call(
        functools.partial(_dsa_body, tq=tq, n_top=n_top, idx_bits=max(1, (t - 1).bit_length())),
        grid=(batch, nq),
        in_specs=[rowspec(IQW), rowspec(LANES), seqspec(IQW), rowspec(QW), seqspec(2 * KW)],
        out_specs=rowspec(QW),
        out_shape=jax.ShapeDtypeStruct(q.shape, MXU_DTYPE),
        scratch_shapes=[pltpu.VMEM((nq, tq, tq), jnp.int32), pltpu.VMEM((tq, LANES), jnp.int32),
                        pltpu.VMEM((N_KV_HEADS, GROUP * tq, LANES), f32)],
        compiler_params=pltpu.CompilerParams(dimension_semantics=("parallel", "arbitrary"),
                                             vmem_limit_bytes=VMEM_LIMIT),
        name="dsa",
    )(iq, ikw, ikr, q, kv)


def _decode_body(pt_ref, ca_ref, cc_ref, ci_ref, qa_ref, qc_ref, iq_ref, iw_ref, na_ref, nc_ref, ni_ref,
                 oa_ref, oc_ref, ka_sc, va_sc, kc_sc, vc_sc, ik_sc, *, page, n_top, idx_bits):
    del pt_ref
    p = pl.program_id(1)
    past = ka_sc.shape[0]
    nbp = past // MOBA_BLOCK
    rows = pl.ds(pl.multiple_of(p * page, page), page)
    ka_sc[rows, :] = ca_ref[:, 0:KW].astype(MXU_DTYPE)
    va_sc[rows, :] = ca_ref[:, KW:2 * KW].astype(MXU_DTYPE)
    kc_sc[rows, :] = cc_ref[:, 0:KW].astype(MXU_DTYPE)
    vc_sc[rows, :] = cc_ref[:, KW:2 * KW].astype(MXU_DTYPE)
    ik_sc[rows, :] = ci_ref[...].astype(MXU_DTYPE)

    @pl.when(p == pl.num_programs(1) - 1)
    def _():
        lane = lax.broadcasted_iota(jnp.int32, (N_HEADS, LANES), 1)

        def finish(s_list, mask_list, q8, knew, vnew, v_sc, extra_ok):
            s_new = jnp.sum(q8.astype(f32) * knew, axis=1, keepdims=True)
            s_new = jnp.where(extra_ok, s_new, NEG_INF)
            sm = [jnp.where(mk, s, NEG_INF) for s, mk in zip(s_list, mask_list)]
            m = s_new
            for s in sm:
                m = jnp.maximum(m, jnp.max(s, axis=1, keepdims=True))
            p_new = jnp.where(extra_ok, jnp.exp(s_new - m), 0.0)
            l = p_new
            acc = p_new * vnew
            w = s_list[0].shape[1]
            for c, (s, mk) in enumerate(zip(sm, mask_list)):
                pc = jnp.where(mk, jnp.exp(s - m), 0.0)
                l = l + jnp.sum(pc, axis=1, keepdims=True)
                acc = acc + _dot(pc.astype(MXU_DTYPE), v_sc[c * w:(c + 1) * w, :])
            return acc / l

        q8 = qa_ref[...]
        sa = [_nt_dot(q8, ka_sc[n * MOBA_BLOCK:(n + 1) * MOBA_BLOCK, :]) for n in range(nbp)]
        gate = jnp.full((N_HEADS, LANES), NEG_INF, f32)
        for n in range(nbp):
            gate = jnp.where(lane == n, jnp.sum(sa[n], axis=1, keepdims=True) * (1.0 / MOBA_BLOCK), gate)
        sel = jnp.zeros((N_HEADS, LANES), f32)
        g = gate
        for _ in range(min(MOBA_TOPK, nbp + 1)):
            mx = jnp.max(g, axis=1, keepdims=True)
            first = jnp.min(jnp.where(g == mx, lane, 1 << 20), axis=1, keepdims=True)
            hit = lane == first
            sel = jnp.where(hit & (mx > 0.5 * NEG_INF), 1.0, sel)
            g = jnp.where(hit, -3e38, g)
        masks = [jnp.broadcast_to(jnp.max(jnp.where(lane == n, sel, 0.0), axis=1, keepdims=True) > 0.5,
                                  sa[n].shape) for n in range(nbp)]
        oa_ref[...] = finish(sa, masks, q8, na_ref[:, 0:KW], na_ref[:, KW:2 * KW], va_sc, True)

        sc_scale = (IDX_DIM ** -0.5) * (IDX_HEADS ** -0.5)
        iq8 = iq_ref[...]
        iw8 = iw_ref[...]
        rel = jnp.maximum(_nt_dot(iq8, ik_sc[...]), 0.0)
        sc = jnp.sum(iw8 * rel, axis=0, keepdims=True) * sc_scale
        rel_n = jnp.maximum(jnp.sum(iq8.astype(f32) * ni_ref[...], axis=1, keepdims=True), 0.0)
        sc_n = jnp.sum(iw8 * rel_n, axis=0, keepdims=True) * sc_scale
        key = _sortable(sc)
        key_n = _sortable(sc_n)
        idx = lax.broadcasted_iota(jnp.int32, (1, past), 1)

        def cnt(pred_past, pred_new):
            return (jnp.sum(jnp.where(pred_past, 1, 0), axis=1, keepdims=True) + jnp.where(pred_new, 1, 0))

        thr = _select_threshold(lambda cand: cnt(key >= cand, key_n >= cand), n_top, 1)
        need = n_top - cnt(key > thr, key_n > thr)
        eq = key == thr

        def step(it, t_i):
            cand = t_i | lax.shift_left(jnp.int32(1), jnp.int32(idx_bits - 1) - it)
            c = jnp.sum(jnp.where(eq & (idx < cand), 1, 0), axis=1, keepdims=True)
            return jnp.where(c < need, cand, t_i)
        cut = lax.fori_loop(0, idx_bits, step, jnp.zeros((1, 1), jnp.int32))
        eq_past = jnp.sum(jnp.where(eq, 1, 0), axis=1, keepdims=True)
        keep = (key > thr) | (eq & (idx <= cut))
        keep_n = (key_n > thr) | ((key_n == thr) & (eq_past < need))
        q8c = qc_ref[...]
        s_c = _nt_dot(q8c, kc_sc[...])
        oc_ref[...] = finish([s_c], [jnp.broadcast_to(keep, s_c.shape)], q8c,
                             nc_ref[:, 0:KW], nc_ref[:, KW:2 * KW], vc_sc, keep_n)


def _decode_call(page_table, cache_a, cache_c, cache_i, layer, q8a, q8c, iq8, iw8, kva, kvc, ik):
    db, n_pages = page_table.shape
    page = cache_a.shape[2]
    past = n_pages * page
    assert past % MOBA_BLOCK == 0
    n_top = min(DSA_TOPK, (past + 1) // 4)
    assert n_top <= past
    pagespec = lambda w: pl.BlockSpec((None, None, page, w), lambda b, p, pt: (layer, pt[b, p], 0, 0))
    seqspec = lambda r, w: pl.BlockSpec((None, r, w), lambda b, p, pt: (b, 0, 0))
    grid_spec = pltpu.PrefetchScalarGridSpec(
        num_scalar_prefetch=1,
        grid=(db, n_pages),
        in_specs=[pagespec(2 * KW), pagespec(2 * KW), pagespec(IDX_DIM),
                  seqspec(N_HEADS, KW), seqspec(N_HEADS, KW), seqspec(IDX_HEADS, IDX_DIM), seqspec(IDX_HEADS, 1),
                  seqspec(1, 2 * KW), seqspec(1, 2 * KW), seqspec(1, IDX_DIM)],
        out_specs=[seqspec(N_HEADS, KW), seqspec(N_HEADS, KW)],
        scratch_shapes=[pltpu.VMEM((past, KW), MXU_DTYPE)] * 4 + [pltpu.VMEM((past, IDX_DIM), MXU_DTYPE)],
    )
    return pl.pallas_call(
        functools.partial(_decode_body, page=page, n_top=n_top, idx_bits=max(1, past.bit_length())),
        grid_spec=grid_spec,
        out_shape=[jax.ShapeDtypeStruct((db, N_HEADS, KW), f32)] * 2,
        compiler_params=pltpu.CompilerParams(dimension_semantics=("parallel", "arbitrary"),
                                             vmem_limit_bytes=VMEM_LIMIT),
        name="decode",
    )(page_table, cache_a, cache_c, cache_i, q8a, q8c, iq8, iw8, kva, kvc, ik)


def _merge_body(x_ref, oa_ref, ob_ref, oc_ref, wgate_ref, wpa_ref, wpb_ref, wpc_ref, wout_ref, g_ref, b_ref, o_ref):
    x = x_ref[...]
    d = x.shape[1]
    xb = x.astype(MXU_DTYPE)
    merged = None
    for k, (o_r, wp_r) in enumerate(((oa_ref, wpa_ref), (ob_ref, wpb_ref), (oc_ref, wpc_ref))):
        gate = _sigmoid(_dot(xb, wgate_ref[:, k * d:(k + 1) * d]))
        term = gate * _dot(o_r[...].astype(MXU_DTYPE), wp_r[...])
        merged = term if merged is None else merged + term
    y = DEEPNORM_ALPHA * x + _dot(merged.astype(MXU_DTYPE), wout_ref[...])
    o_ref[...] = _layer_norm(y, g_ref[...], b_ref[...])


def _merge_call(x, oa, ob, oc, wgate, wpa, wpb, wpc, wout, g, b, *, tm):
    n, d = x.shape
    row = lambda w: pl.BlockSpec((tm, w), lambda i: (i, 0))
    return pl.pallas_call(
        _merge_body,
        grid=(n // tm,),
        in_specs=[row(d), row(BRANCH_WIDTH), row(BRANCH_WIDTH), row(BRANCH_WIDTH)]
        + [_const_spec(a.shape) for a in (wgate, wpa, wpb, wpc, wout, g, b)],
        out_specs=row(d),
        out_shape=jax.ShapeDtypeStruct((n, d), f32),
        compiler_params=pltpu.CompilerParams(dimension_semantics=("parallel",), vmem_limit_bytes=VMEM_LIMIT),
        name="merge",
    )(x, oa, ob, oc, wgate, wpa, wpb, wpc, wout, g, b)


def _moe_body(x_ref, wrh_ref, wrl_ref, br_ref, w1_ref, w3_ref, w2_ref, g_ref, b_ref, o_ref, hh_sc):
    x = x_ref[...]
    tm = x.shape[0]
    xb = x.astype(MXU_DTYPE)
    xl = (x - xb.astype(f32)).astype(MXU_DTYPE)
    logits = _dot(xb, wrh_ref[...]) + _dot(xb, wrl_ref[...]) + _dot(xl, wrh_ref[...]) + br_ref[...]
    lane = lax.broadcasted_iota(jnp.int32, (tm, LANES), 1)
    big = 1 << 20
    gl = jnp.where(lane < N_GROUPS, logits, -jnp.inf)
    gmax = jnp.max(gl, axis=1, keepdims=True)
    g_sel = jnp.min(jnp.where(gl == gmax, lane, big), axis=1, keepdims=True)
    g_w = 1.0 / jnp.sum(jnp.exp(gl - gmax), axis=1, keepdims=True)
    lo = N_GROUPS + EXPERTS_PER_GROUP * g_sel
    el = jnp.where((lane >= lo) & (lane < lo + EXPERTS_PER_GROUP), logits, -jnp.inf)
    m1 = jnp.max(el, axis=1, keepdims=True)
    i1 = jnp.min(jnp.where(el == m1, lane, big), axis=1, keepdims=True)
    el2 = jnp.where(lane == i1, -jnp.inf, el)
    m2 = jnp.max(el2, axis=1, keepdims=True)
    i2 = jnp.min(jnp.where(el2 == m2, lane, big), axis=1, keepdims=True)
    e2 = jnp.exp(m2 - m1)
    w_first = g_w / (1.0 + e2)
    w_second = g_w * e2 / (1.0 + e2)
    comb = jnp.where(lane == i1, w_first, 0.0) + jnp.where(lane == i2, w_second, 0.0)

    ff = w1_ref.shape[2]
    for e in range(N_EXPERTS):
        hg = _dot(xb, w1_ref[e])
        hu = _dot(xb, w3_ref[e])
        ce = comb[:, N_GROUPS + e:N_GROUPS + e + 1]
        hh_sc[:, e * ff:(e + 1) * ff] = (hg * _sigmoid(hg) * hu * ce).astype(hh_sc.dtype)
    y = DEEPNORM_ALPHA * x + _dot(hh_sc[...], w2_ref[...])
    o_ref[...] = _layer_norm(y, g_ref[...], b_ref[...])


def _moe_call(x, wrh, wrl, br, w1, w3, w2, g, b, *, tm):
    n, d = x.shape
    row = pl.BlockSpec((tm, d), lambda i: (i, 0))
    return pl.pallas_call(
        _moe_body,
        grid=(n // tm,),
        in_specs=[row] + [_const_spec(a.shape) for a in (wrh, wrl, br, w1, w3, w2, g, b)],
        out_specs=row,
        out_shape=jax.ShapeDtypeStruct((n, d), f32),
        scratch_shapes=[pltpu.VMEM((tm, w2.shape[0]), MXU_DTYPE)],
        compiler_params=pltpu.CompilerParams(dimension_semantics=("parallel",), vmem_limit_bytes=VMEM_LIMIT),
        name="moe",
    )(x, wrh, wrl, br, w1, w3, w2, g, b)


def _rope_table(pos, dim, ident_from=None):
    rot = dim // ROPE_FRACTION
    half = rot // 2
    inv_freq = ROPE_THETA ** (-jnp.arange(half, dtype=f32) / half)
    ang = pos.astype(f32)[:, None] * inv_freq[None, :]
    cos, sin = jnp.cos(ang), jnp.sin(ang)
    n = pos.shape[0]
    z = lambda w: jnp.zeros((n, w), f32)
    c = jnp.concatenate([cos, cos, jnp.ones((n, dim - rot), f32)], axis=1)
    sa = jnp.concatenate([z(half), sin, z(dim - rot)], axis=1)
    sb = jnp.concatenate([-sin, z(dim - half)], axis=1)
    tab = jnp.stack([jnp.tile(a, (1, LANES // dim)) for a in (c, sa, sb)])
    if ident_from is not None:
        keep = (jnp.arange(LANES) < ident_from)[None, None, :]
        ident = jnp.stack([jnp.ones((n, LANES), f32), jnp.zeros((n, LANES), f32), jnp.zeros((n, LANES), f32)])
        tab = jnp.where(keep, tab, ident)
    return tab


def _layer_params(l, w_in, w_pa, w_pb, w_pc, w_out, w_s, b_s, ln_v_g, ln_v_b, ln1_g, ln1_b, ln2_g, ln2_b,
                  w_rg, b_rg, w_re, b_re, w_e1, w_e3, w_e2):
    d = w_in.shape[1]
    w = w_in[l]
    offs = {}
    o = 0
    for name, width in (("aq", QW), ("ak", KW), ("av", KW), ("cq", QW), ("ck", KW), ("cv", KW),
                        ("iq", IQW), ("ik", IDX_DIM), ("iw", IDX_HEADS), ("gu", GMLP_WIDTH), ("gv", GMLP_WIDTH),
                        ("ga", d), ("gb", d), ("gc", d)):
        offs[name] = (o, o + width)
        o += width
    assert o == w.shape[1]
    cols = lambda n: w[:, offs[n][0]:offs[n][1]]
    pad = jnp.zeros((d, LANES - IDX_DIM - IDX_HEADS), w.dtype)
    wm = jnp.concatenate([cols("aq"), cols("cq"), cols("ak"), cols("ck"), cols("av"), cols("cv"),
                          cols("iq"), cols("ik"), cols("iw"), pad], axis=1).astype(MXU_DTYPE)
    wg = jnp.concatenate([cols("gu"), cols("gv")], axis=1).astype(MXU_DTYPE)
    wgate = jnp.concatenate([cols("ga"), cols("gb"), cols("gc")], axis=1).astype(MXU_DTYPE)
    causal = jnp.tril(jnp.ones((GMLP_CHUNK, GMLP_CHUNK), dtype=bool))
    gd = GMLP_WIDTH // GMLP_GROUPS
    ws = jnp.where(causal[None], w_s[l], 0)
    wr = jnp.concatenate([w_rg[l], w_re[l], jnp.zeros((d, LANES - N_GROUPS - N_EXPERTS), f32)], axis=1)
    wrh = wr.astype(MXU_DTYPE)
    row = lambda a: a.reshape(1, -1)
    return dict(
        wm=wm, wg=wg, wgate=wgate,
        ws_chunk=ws.astype(MXU_DTYPE),
        bs_chunk=jnp.repeat(b_s[l].T, gd, axis=1),
        ws_row=jnp.repeat(w_s[l][:, 0, 0], gd).reshape(1, -1),
        bs_row=jnp.repeat(b_s[l][:, 0], gd).reshape(1, -1),
        lnv_g=row(ln_v_g[l]), lnv_b=row(ln_v_b[l]),
        wpa=w_pa[l].astype(MXU_DTYPE), wpb=w_pb[l].astype(MXU_DTYPE), wpc=w_pc[l].astype(MXU_DTYPE),
        wout=w_out[l].astype(MXU_DTYPE), ln1_g=row(ln1_g[l]), ln1_b=row(ln1_b[l]),
        wrh=wrh, wrl=(wr - wrh.astype(f32)).astype(MXU_DTYPE),
        br=jnp.concatenate([b_rg[l], b_re[l], jnp.zeros((LANES - N_GROUPS - N_EXPERTS,), f32)]).reshape(1, -1),
        w1=w_e1[l].astype(MXU_DTYPE), w3=w_e3[l].astype(MXU_DTYPE),
        w2=w_e2[l].reshape(-1, d).astype(MXU_DTYPE), ln2_g=row(ln2_g[l]), ln2_b=row(ln2_b[l]),
    )


def kernel(x_prompt, x_sample, cache_a_kv, cache_c_kv, cache_c_kidx, page_table, w_in, w_pa, w_pb, w_pc, w_out, w_s, b_s, ln_v_g, ln_v_b, ln1_g, ln1_b, ln2_g, ln2_b, w_rg, b_rg, w_re, b_re, w_e1, w_e3, w_e2):
    bp, tp, d = x_prompt.shape
    db, ts, _ = x_sample.shape
    depth = w_in.shape[0]
    n_phys, page = cache_a_kv.shape[1], cache_a_kv.shape[2]
    n_pages = page_table.shape[1]
    past = n_pages * page
    assert ts == 1 and tp % MOBA_BLOCK == 0 and cache_a_kv.shape[3:] == (2, N_KV_HEADS, HEAD_DIM)

    cache_a = cache_a_kv.reshape(depth, n_phys, page, 2 * KW)
    cache_c = cache_c_kv.reshape(depth, n_phys, page, 2 * KW)
    page_table = page_table.astype(jnp.int32)

    tm_p = 512 if tp % 512 == 0 else MOBA_BLOCK
    tm_s = db
    pos_p = jnp.arange(tp, dtype=jnp.int32)
    pos_s = jnp.full((db,), past, jnp.int32)
    tabs_p = (_rope_table(pos_p, HEAD_DIM), _rope_table(pos_p, IDX_DIM), _rope_table(pos_p, IDX_DIM, IDX_DIM))
    tabs_s = (_rope_table(pos_s, HEAD_DIM), _rope_table(pos_s, IDX_DIM), _rope_table(pos_s, IDX_DIM, IDX_DIM))

    xp = x_prompt.reshape(bp * tp, d)
    xs = x_sample.reshape(db, d)
    outs = [[] for _ in range(7)]
    eye = jnp.eye(N_KV_HEADS, dtype=f32)
    for l in range(depth):
        p = _layer_params(l, w_in, w_pa, w_pb, w_pc, w_out, w_s, b_s, ln_v_g, ln_v_b, ln1_g, ln1_b, ln2_g, ln2_b,
                          w_rg, b_rg, w_re, b_re, w_e1, w_e3, w_e2)

        qa, qc, kva, kvc, kvab, kvcb, iq, ikw, ikr, ob, _ = _proj_call(
            xp, p["wm"], p["wg"], *tabs_p, p["lnv_g"], p["lnv_b"], p["ws_chunk"], p["bs_chunk"],
            tm=tm_p, chunked=True)
        oa = _moba_call(qa, kvab, batch=bp, t=tp)
        oc = _dsa_call(iq, ikw, ikr, qc, kvcb, batch=bp, t=tp)
        x1 = _merge_call(xp, oa, ob, oc, p["wgate"], p["wpa"], p["wpb"], p["wpc"], p["wout"],
                         p["ln1_g"], p["ln1_b"], tm=tm_p)
        xp = _moe_call(x1, p["wrh"], p["wrl"], p["br"], p["w1"], p["w3"], p["w2"], p["ln2_g"], p["ln2_b"], tm=tm_p)
        outs[0].append(kva.reshape(bp, tp // page, page, 2, N_KV_HEADS, HEAD_DIM))
        outs[1].append(kvc.reshape(bp, tp // page, page, 2, N_KV_HEADS, HEAD_DIM))
        outs[2].append(ikw[:, 0:IDX_DIM].reshape(bp, tp // page, page, IDX_DIM))

        qa, qc, kva, kvc, _, _, iq, ikw, _, ob, vv = _proj_call(
            xs, p["wm"], p["wg"], *tabs_s, p["lnv_g"], p["lnv_b"], p["ws_row"], p["bs_row"],
            tm=tm_s, chunked=False)
        place = lambda q: jnp.einsum("bkgd,kj->bkgjd", q.astype(f32).reshape(db, N_KV_HEADS, GROUP, HEAD_DIM),
                                     eye).reshape(db, N_HEADS, KW).astype(MXU_DTYPE)
        ik_new = ikw[:, 0:IDX_DIM]
        o8a, o8c = _decode_call(
            page_table, cache_a, cache_c, cache_c_kidx, l, place(qa), place(qc),
            iq.reshape(db, IDX_HEADS, IDX_DIM), ikw[:, IDX_DIM:IDX_DIM + IDX_HEADS].reshape(db, IDX_HEADS, 1),
            kva.reshape(db, 1, 2 * KW), kvc.reshape(db, 1, 2 * KW), ik_new.reshape(db, 1, IDX_DIM))
        unplace = lambda o: jnp.einsum("bkgkd->bkgd", o.reshape(db, N_KV_HEADS, GROUP, N_KV_HEADS, HEAD_DIM)
                                       ).reshape(db, QW).astype(MXU_DTYPE)
        x1 = _merge_call(xs, unplace(o8a), ob, unplace(o8c), p["wgate"], p["wpa"], p["wpb"], p["wpc"], p["wout"],
                         p["ln1_g"], p["ln1_b"], tm=tm_s)
        xs = _moe_call(x1, p["wrh"], p["wrl"], p["br"], p["w1"], p["w3"], p["w2"], p["ln2_g"], p["ln2_b"], tm=tm_s)
        outs[3].append(kva.reshape(db, 1, 2, N_KV_HEADS, HEAD_DIM))
        outs[4].append(kvc.reshape(db, 1, 2, N_KV_HEADS, HEAD_DIM))
        outs[5].append(ik_new.reshape(db, 1, IDX_DIM))
        outs[6].append(vv.reshape(db, 1, GMLP_WIDTH))

    return (xp.reshape(bp, tp, d), xs.reshape(db, 1, d)) + tuple(jnp.stack(o) for o in outs)
```

```python
import functools
import math

import jax
import jax.numpy as jnp
from jax import lax
from jax.experimental import pallas as pl
from jax.experimental.pallas import tpu as pltpu

HEAD_DIM = 64
N_HEADS = 8
N_KV_HEADS = 4
GROUP = N_HEADS // N_KV_HEADS
MOBA_BLOCK = 256
MOBA_TOPK = 3
IDX_HEADS = 8
IDX_DIM = 32
DSA_TOPK = 256
GMLP_CHUNK = 128
GMLP_GROUPS = 8
GMLP_WIDTH = 512
BRANCH_WIDTH = 512
N_GROUPS = 4
EXPERTS_PER_GROUP = 4
N_EXPERTS = N_GROUPS * EXPERTS_PER_GROUP
ROPE_THETA = 500000.0
ROPE_FRACTION = 4
LN_EPS = 1e-5
NEG_INF = -1e30
DEEPNORM_ALPHA = (2 * 4) ** 0.25

LANES = 128
MXU_DTYPE = jnp.bfloat16
VMEM_LIMIT = 56 * 1024 * 1024

QW = N_HEADS * HEAD_DIM
KW = N_KV_HEADS * HEAD_DIM
IQW = IDX_HEADS * IDX_DIM
INT_MIN = -2147483648
HALF_NEG_KEY = -1892283083

f32 = jnp.float32


def _nt_dot(a, b):
    return lax.dot_general(a, b, (((1,), (1,)), ((), ())), preferred_element_type=f32)


def _dot(a, b):
    return jnp.dot(a, b, preferred_element_type=f32)


def _layer_norm(y, g, b):
    mu = jnp.mean(y, axis=-1, keepdims=True)
    d = y - mu
    var = jnp.mean(d * d, axis=-1, keepdims=True)
    return d * lax.rsqrt(var + LN_EPS) * g + b


def _gelu_tanh(x):
    cdf = 0.5 * (1.0 + jnp.tanh(math.sqrt(2.0 / math.pi) * (x + 0.044715 * (x * x * x))))
    return x * cdf


def _sigmoid(x):
    return 1.0 / (1.0 + jnp.exp(-x))


def _sortable(x):
    bits = pltpu.bitcast(x, jnp.int32)
    return jnp.where(bits < 0, bits ^ jnp.int32(0x7FFFFFFF), bits)


def _const_spec(shape):
    nd = len(shape)
    return pl.BlockSpec(shape, lambda *_: (0,) * nd, pipeline_mode=pl.Buffered(1))


def _rope_chunk(h, c, sa, sb, half):
    return h * c + pltpu.roll(h, half, 1) * sa + pltpu.roll(h, LANES - half, 1) * sb


def _proj_body(x_ref, wm_ref, wg_ref, t64_ref, t32_ref, tl_ref, lng_ref, lnb_ref, ws_ref, bs_ref,
               qa_ref, qc_ref, kva_ref, kvc_ref, kvab_ref, kvcb_ref, iq_ref, ikw_ref, ikr_ref, ob_ref, vv_ref,
               *, chunked):
    xb = x_ref[...].astype(MXU_DTYPE)
    tm = xb.shape[0]
    c64, sa64, sb64 = t64_ref[0], t64_ref[1], t64_ref[2]
    c32, sa32, sb32 = t32_ref[0], t32_ref[1], t32_ref[2]
    cl, sal, sbl = tl_ref[0], tl_ref[1], tl_ref[2]
    h64 = (HEAD_DIM // ROPE_FRACTION) // 2
    h32 = (IDX_DIM // ROPE_FRACTION) // 2
    scale = HEAD_DIM ** -0.5

    hq = _dot(xb, wm_ref[:, 0:2 * QW])
    for c in range(2 * QW // LANES):
        r = _rope_chunk(hq[:, c * LANES:(c + 1) * LANES], c64, sa64, sb64, h64) * scale
        dst = qa_ref if c < QW // LANES else qc_ref
        cc = c % (QW // LANES)
        dst[:, cc * LANES:(cc + 1) * LANES] = r.astype(dst.dtype)

    o = 2 * QW
    hk = _dot(xb, wm_ref[:, o:o + 2 * KW])
    hv = _dot(xb, wm_ref[:, o + 2 * KW:o + 4 * KW])
    for c in range(2 * KW // LANES):
        r = _rope_chunk(hk[:, c * LANES:(c + 1) * LANES], c64, sa64, sb64, h64)
        v = hv[:, c * LANES:(c + 1) * LANES]
        dst, dstb = (kva_ref, kvab_ref) if c < KW // LANES else (kvc_ref, kvcb_ref)
        cc = c % (KW // LANES)
        dst[:, cc * LANES:(cc + 1) * LANES] = r
        dst[:, KW + cc * LANES:KW + (cc + 1) * LANES] = v
        dstb[:, cc * LANES:(cc + 1) * LANES] = r.astype(dstb.dtype)
        dstb[:, KW + cc * LANES:KW + (cc + 1) * LANES] = v.astype(dstb.dtype)

    o = 2 * QW + 4 * KW
    hi = _dot(xb, wm_ref[:, o:o + IQW + LANES])
    for c in range(IQW // LANES):
        r = _rope_chunk(hi[:, c * LANES:(c + 1) * LANES], c32, sa32, sb32, h32)
        iq_ref[:, c * LANES:(c + 1) * LANES] = r.astype(iq_ref.dtype)
    last = _rope_chunk(hi[:, IQW:IQW + LANES], cl, sal, sbl, h32)
    ikw_ref[...] = last
    lane = lax.broadcasted_iota(jnp.int32, (tm, LANES), 1)
    t = jnp.where(lane < IDX_DIM, last, 0.0)
    t = t + pltpu.roll(t, IDX_DIM, 1)
    t = t + pltpu.roll(t, 2 * IDX_DIM, 1)
    for c in range(IQW // LANES):
        ikr_ref[:, c * LANES:(c + 1) * LANES] = t.astype(ikr_ref.dtype)

    hg = _dot(xb, wg_ref[...])
    u = _gelu_tanh(hg[:, 0:GMLP_WIDTH])
    vv = _layer_norm(_gelu_tanh(hg[:, GMLP_WIDTH:2 * GMLP_WIDTH]), lng_ref[...], lnb_ref[...])
    vv_ref[...] = vv
    if chunked:
        vb = vv.astype(MXU_DTYPE)
        gd = GMLP_WIDTH // GMLP_GROUPS
        lane_c = lax.broadcasted_iota(jnp.int32, (GMLP_CHUNK, LANES), 1)
        for c in range(tm // GMLP_CHUNK):
            rows = slice(c * GMLP_CHUNK, (c + 1) * GMLP_CHUNK)
            for pr in range(GMLP_WIDTH // LANES):
                vp = vb[rows, pr * LANES:(pr + 1) * LANES]
                g0 = (pr * LANES) // gd
                r0 = _dot(ws_ref[g0], vp)
                r1 = _dot(ws_ref[g0 + 1], vp)
                s = jnp.where(lane_c < gd, r0, r1) + bs_ref[:, pr * LANES:(pr + 1) * LANES]
                ob_ref[rows, pr * LANES:(pr + 1) * LANES] = (u[rows, pr * LANES:(pr + 1) * LANES] * s).astype(ob_ref.dtype)
    else:
        ob_ref[...] = (u * (vv * ws_ref[...] + bs_ref[...])).astype(ob_ref.dtype)


def _proj_call(x, wm, wg, t64, t32, tl, lng, lnb, ws, bs, *, tm, chunked):
    n, d = x.shape
    nt = t64.shape[1] // tm
    grid = (n // tm,)
    row = lambda w: pl.BlockSpec((tm, w), lambda i: (i, 0))
    tab = pl.BlockSpec((3, tm, LANES), lambda i: (0, i % nt, 0))
    outs = [
        (QW, MXU_DTYPE), (QW, MXU_DTYPE),
        (2 * KW, f32), (2 * KW, f32),
        (2 * KW, MXU_DTYPE), (2 * KW, MXU_DTYPE),
        (IQW, MXU_DTYPE),
        (LANES, f32),
        (IQW, MXU_DTYPE),
        (GMLP_WIDTH, MXU_DTYPE),
        (GMLP_WIDTH, f32),
    ]
    return pl.pallas_call(
        functools.partial(_proj_body, chunked=chunked),
        grid=grid,
        in_specs=[row(d), _const_spec(wm.shape), _const_spec(wg.shape), tab, tab, tab,
                  _const_spec(lng.shape), _const_spec(lnb.shape), _const_spec(ws.shape), _const_spec(bs.shape)],
        out_specs=[row(w) for w, _ in outs],
        out_shape=[jax.ShapeDtypeStruct((n, w), dt) for w, dt in outs],
        compiler_params=pltpu.CompilerParams(dimension_semantics=("parallel",), vmem_limit_bytes=VMEM_LIMIT),
        name="proj",
    )(x, wm, wg, t64, t32, tl, lng, lnb, ws, bs)


def _head_rows(q_ref, kvh, tq):
    qp = q_ref[:, kvh * LANES:(kvh + 1) * LANES].astype(f32)
    rl = pltpu.roll(qp, HEAD_DIM, 1)
    lo = lax.broadcasted_iota(jnp.int32, (tq, LANES), 1) < HEAD_DIM
    if kvh % 2 == 0:
        g0, g1 = jnp.where(lo, qp, 0.0), jnp.where(lo, rl, 0.0)
    else:
        g0, g1 = jnp.where(lo, 0.0, rl), jnp.where(lo, 0.0, qp)
    return jnp.concatenate([g0, g1], axis=0).astype(MXU_DTYPE)


def _head_out(acc, l, kvh, tq):
    o = acc / l
    o0, o1 = o[0:tq], o[tq:2 * tq]
    lo = lax.broadcasted_iota(jnp.int32, (tq, LANES), 1) < HEAD_DIM
    if kvh % 2 == 0:
        return jnp.where(lo, o0, pltpu.roll(o1, HEAD_DIM, 1))
    return jnp.where(lo, pltpu.roll(o0, HEAD_DIM, 1), o1)


def _softmax_step(s, mask, m, l, acc, v_t):
    s = jnp.where(mask, s, NEG_INF)
    m_new = jnp.maximum(m, jnp.max(s, axis=1, keepdims=True))
    p = jnp.where(mask, jnp.exp(s - m_new), 0.0)
    alpha = jnp.exp(m - m_new)
    l_new = alpha * l + jnp.sum(p, axis=1, keepdims=True)
    acc_new = alpha * acc + _dot(p.astype(MXU_DTYPE), v_t)
    return m_new, l_new, acc_new


def _moba_body(q_ref, kv_ref, o_ref, kmean_sc, acc_sc, *, tq):
    i = pl.program_id(1)
    t = kv_ref.shape[0]
    nb = t // MOBA_BLOCK
    rows = GROUP * tq

    @pl.when(i == 0)
    def _():
        kmean_sc[...] = jnp.zeros_like(kmean_sc)
        for n in range(nb):
            blk = kv_ref[n * MOBA_BLOCK:(n + 1) * MOBA_BLOCK, 0:KW].astype(f32)
            kmean_sc[n:n + 1, :] = jnp.mean(blk, axis=0, keepdims=True)

    lane = lax.broadcasted_iota(jnp.int32, (rows, LANES), 1)
    key_in_blk = lax.broadcasted_iota(jnp.int32, (rows, MOBA_BLOCK), 1)
    row_in_blk = lax.broadcasted_iota(jnp.int32, (rows, MOBA_BLOCK), 0) % tq
    causal = key_in_blk <= row_in_blk

    for kvh in range(N_KV_HEADS):
        pc = kvh // 2
        ks = slice(pc * LANES, (pc + 1) * LANES)
        vs = slice(KW + pc * LANES, KW + (pc + 1) * LANES)
        qh = _head_rows(q_ref, kvh, tq)

        gate = _nt_dot(qh, kmean_sc[:, ks].astype(MXU_DTYPE))
        g = jnp.where(lane < i, gate, NEG_INF)
        sel = jnp.zeros((rows, LANES), f32)
        for _ in range(MOBA_TOPK):
            mx = jnp.max(g, axis=1, keepdims=True)
            first = jnp.min(jnp.where(g == mx, lane, 1 << 20), axis=1, keepdims=True)
            hit = lane == first
            sel = jnp.where(hit & (mx > 0.5 * NEG_INF), 1.0, sel)
            g = jnp.where(hit, -3e38, g)

        own = pl.multiple_of(i * MOBA_BLOCK, MOBA_BLOCK)
        s = _nt_dot(qh, kv_ref[pl.ds(own, MOBA_BLOCK), ks])
        m0 = jnp.full((rows, 1), NEG_INF, f32)
        l0 = jnp.zeros((rows, 1), f32)
        m, l, acc = _softmax_step(s, causal, m0, l0, jnp.zeros((rows, LANES), f32),
                                  kv_ref[pl.ds(own, MOBA_BLOCK), vs])
        acc_sc[...] = acc

        def past(n, carry):
            m, l = carry
            start = pl.multiple_of(n * MOBA_BLOCK, MOBA_BLOCK)
            s = _nt_dot(qh, kv_ref[pl.ds(start, MOBA_BLOCK), ks])
            chosen = jnp.max(jnp.where(lane == n, sel, 0.0), axis=1, keepdims=True) > 0.5
            mask = jnp.broadcast_to(chosen, s.shape)
            m, l, acc = _softmax_step(s, mask, m, l, acc_sc[...], kv_ref[pl.ds(start, MOBA_BLOCK), vs])
            acc_sc[...] = acc
            return m, l

        m, l = lax.fori_loop(0, i, past, (m, l))
        o_ref[:, kvh * LANES:(kvh + 1) * LANES] = _head_out(acc_sc[...], l, kvh, tq).astype(o_ref.dtype)


def _moba_call(q, kv, *, batch, t):
    tq = MOBA_BLOCK
    nq = t // tq
    return pl.pallas_call(
        functools.partial(_moba_body, tq=tq),
        grid=(batch, nq),
        in_specs=[pl.BlockSpec((tq, QW), lambda b, i: (b * nq + i, 0)),
                  pl.BlockSpec((t, 2 * KW), lambda b, i: (b, 0))],
        out_specs=pl.BlockSpec((tq, QW), lambda b, i: (b * nq + i, 0)),
        out_shape=jax.ShapeDtypeStruct(q.shape, MXU_DTYPE),
        scratch_shapes=[pltpu.VMEM((LANES, KW), f32), pltpu.VMEM((GROUP * tq, LANES), f32)],
        compiler_params=pltpu.CompilerParams(dimension_semantics=("parallel", "arbitrary"),
                                             vmem_limit_bytes=VMEM_LIMIT),
        name="moba",
    )(q, kv)


def _select_threshold(count_ge, n_top, rows):
    def step(it, t_u):
        bit = lax.shift_left(jnp.int32(1), jnp.int32(31) - it)
        cand_u = t_u | bit
        cnt = count_ge(cand_u ^ jnp.int32(INT_MIN))
        return jnp.where(cnt >= n_top, cand_u, t_u)
    t_u = lax.fori_loop(0, 32, step, jnp.zeros((rows, 1), jnp.int32))
    return t_u ^ jnp.int32(INT_MIN)


def _dsa_body(iq_ref, ikw_ref, ikr_ref, q_ref, kv_ref, o_ref, key_sc, cut_sc, acc_sc, *, tq, n_top, idx_bits):
    i = pl.program_id(1)
    ntile = i + 1
    rows = GROUP * tq
    lane_q = lax.broadcasted_iota(jnp.int32, (tq, IQW), 1)
    col = lax.broadcasted_iota(jnp.int32, (tq, tq), 1)
    rowi = lax.broadcasted_iota(jnp.int32, (tq, tq), 0)
    half_neg = jnp.int32(HALF_NEG_KEY)

    iqv = iq_ref[...]
    iqm = [jnp.where((lane_q >= h * IDX_DIM) & (lane_q < (h + 1) * IDX_DIM), iqv, jnp.zeros_like(iqv))
           for h in range(IDX_HEADS)]
    iw = [ikw_ref[:, IDX_DIM + h:IDX_DIM + h + 1] for h in range(IDX_HEADS)]
    sc_scale = (IDX_DIM ** -0.5) * (IDX_HEADS ** -0.5)

    def score_tile(kt, _):
        ik_t = ikr_ref[pl.ds(pl.multiple_of(kt * tq, tq), tq), :]
        sc = jnp.zeros((tq, tq), f32)
        for h in range(IDX_HEADS):
            sc = sc + iw[h] * jnp.maximum(_nt_dot(iqm[h], ik_t), 0.0)
        sc = sc * sc_scale
        sc = jnp.where(kt * tq + col <= i * tq + rowi, sc, NEG_INF)
        key_sc[kt] = _sortable(sc)
        return 0

    lax.fori_loop(0, ntile, score_tile, 0)

    def count_where(pred):
        def body(kt, c):
            return c + jnp.where(pred(key_sc[kt], kt), 1, 0)
        c = lax.fori_loop(0, ntile, body, jnp.zeros((tq, tq), jnp.int32))
        return jnp.sum(c, axis=1, keepdims=True)

    thr = _select_threshold(lambda cand: count_where(lambda k, kt: k >= cand), n_top, tq)
    cnt_gt = count_where(lambda k, kt: k > thr)
    cnt_eq = count_where(lambda k, kt: k == thr)
    need = n_top - cnt_gt

    cut_sc[...] = jnp.full(cut_sc.shape, 1 << 30, jnp.int32)
    excess = (cnt_eq > need) & (thr > half_neg)

    @pl.when(jnp.max(jnp.where(excess, 1, 0)) > 0)
    def _():
        def step(it, t_i):
            cand = t_i | lax.shift_left(jnp.int32(1), jnp.int32(idx_bits - 1) - it)
            cnt = count_where(lambda k, kt: (k == thr) & (kt * tq + col < cand))
            return jnp.where(cnt < need, cand, t_i)
        t_i = lax.fori_loop(0, idx_bits, step, jnp.zeros((tq, 1), jnp.int32))
        cut_sc[...] = jnp.broadcast_to(jnp.where(excess, t_i, 1 << 30), cut_sc.shape)

    cut = cut_sc[:, 0:1]

    qh = [_head_rows(q_ref, kvh, tq) for kvh in range(N_KV_HEADS)]
    acc_sc[...] = jnp.zeros_like(acc_sc)

    def attend(kt, carry):
        start = pl.multiple_of(kt * tq, tq)
        k = key_sc[kt]
        keep = ((k > thr) | ((k == thr) & (kt * tq + col <= cut))) & (k > half_neg)
        mask = jnp.concatenate([keep] * GROUP, axis=0)
        out = []
        for kvh in range(N_KV_HEADS):
            pc = kvh // 2
            m, l = carry[2 * kvh], carry[2 * kvh + 1]
            s = _nt_dot(qh[kvh], kv_ref[pl.ds(start, tq), pc * LANES:(pc + 1) * LANES])
            m, l, acc = _softmax_step(s, mask, m, l, acc_sc[kvh],
                                      kv_ref[pl.ds(start, tq), KW + pc * LANES:KW + (pc + 1) * LANES])
            acc_sc[kvh] = acc
            out += [m, l]
        return tuple(out)

    init = tuple(x for _ in range(N_KV_HEADS)
                 for x in (jnp.full((rows, 1), NEG_INF, f32), jnp.zeros((rows, 1), f32)))
    fin = lax.fori_loop(0, ntile, attend, init)
    for kvh in range(N_KV_HEADS):
        o_ref[:, kvh * LANES:(kvh + 1) * LANES] = _head_out(acc_sc[kvh], fin[2 * kvh + 1], kvh, tq).astype(o_ref.dtype)


def _dsa_call(iq, ikw, ikr, q, kv, *, batch, t):
    tq = 256
    nq = t // tq
    n_top = min(DSA_TOPK, t // 4)
    assert n_top <= tq and t % tq == 0
    rowspec = lambda w: pl.BlockSpec((tq, w), lambda b, i: (b * nq + i, 0))
    seqspec = lambda w: pl.BlockSpec((t, w), lambda b, i: (b, 0))
    return pl.pallas_call(
        functools.partial(_dsa_body, tq=tq, n_top=n_top, idx_bits=max(1, (t - 1).bit_length())),
        grid=(batch, nq),
        in_specs=[rowspec(IQW), rowspec(LANES), seqspec(IQW), rowspec(QW), seqspec(2 * KW)],
        out_specs=rowspec(QW),
        out_shape=jax.ShapeDtypeStruct(q.shape, MXU_DTYPE),
        scratch_shapes=[pltpu.VMEM((nq, tq, tq), jnp.int32), pltpu.VMEM((tq, LANES), jnp.int32),
                        pltpu.VMEM((N_KV_HEADS, GROUP * tq, LANES), f32)],
        compiler_params=pltpu.CompilerParams(dimension_semantics=("parallel", "arbitrary"),
                                             vmem_limit_bytes=VMEM_LIMIT),
        name="dsa",
    )(iq, ikw, ikr, q, kv)


RADIX_BITS = 4


def _radix_select(count_fn, total_bits, top_bit):
    digit = lax.broadcasted_iota(jnp.int32, (1 << RADIX_BITS, 1), 0)
    t = jnp.zeros((1, 1), jnp.int32)
    for step in range(total_bits // RADIX_BITS):
        shift = top_bit + 1 - RADIX_BITS * (step + 1)
        ok = count_fn(t | lax.shift_left(digit, jnp.int32(shift)))
        best = jnp.sum(jnp.where(ok, 1, 0), axis=0, keepdims=True) - 1
        t = t | lax.shift_left(best, jnp.int32(shift))
    return t


def _decode_body(pt_ref, ca_hbm, cc_hbm, ci_hbm, qa_ref, qc_ref, iq_ref, iw_ref, na_ref, nc_ref, ni_ref,
                 oa_ref, oc_ref, abuf, cbuf, ibuf, sems, *, layer, n_pages, page, n_top):
    b = pl.program_id(0)
    slot = lax.rem(b, 2)
    past = n_pages * page
    pages_per_block = MOBA_BLOCK // page
    nbp = n_pages // pages_per_block

    def page_copies(seq, sl):
        out = []
        for p in range(n_pages):
            pg = pt_ref[seq, p]
            out.append(pltpu.make_async_copy(ca_hbm.at[layer, pg], abuf.at[sl, p], sems.at[0, sl]))
            out.append(pltpu.make_async_copy(cc_hbm.at[layer, pg], cbuf.at[sl, p], sems.at[1, sl]))
            out.append(pltpu.make_async_copy(ci_hbm.at[layer, pg], ibuf.at[sl, p], sems.at[2, sl]))
        return out

    @pl.when(b == 0)
    def _():
        for c in page_copies(0, 0):
            c.start()

    @pl.when(b + 1 < pl.num_programs(0))
    def _():
        for c in page_copies(b + 1, 1 - slot):
            c.start()

    for c in page_copies(b, slot):
        c.wait()

    lane = lax.broadcasted_iota(jnp.int32, (N_HEADS, LANES), 1)

    def finish(s_list, mask_list, q8, knew, vnew, vbuf, extra_ok):
        s_new = jnp.sum(q8.astype(f32) * knew, axis=1, keepdims=True)
        s_new = jnp.where(extra_ok, s_new, NEG_INF)
        sm = [jnp.where(mk, s, NEG_INF) for s, mk in zip(s_list, mask_list)]
        m = s_new
        for s in sm:
            m = jnp.maximum(m, jnp.max(s, axis=1, keepdims=True))
        p_new = jnp.where(extra_ok, jnp.exp(s_new - m), 0.0)
        l = p_new
        acc = p_new * vnew
        for p, (s, mk) in enumerate(zip(sm, mask_list)):
            pc = jnp.where(mk, jnp.exp(s - m), 0.0)
            l = l + jnp.sum(pc, axis=1, keepdims=True)
            acc = acc + _nt_dot(pc.astype(MXU_DTYPE), vbuf[slot, p, KW:2 * KW, :].astype(MXU_DTYPE))
        return acc / l

    q8 = qa_ref[...]
    sa = [_dot(q8, abuf[slot, p, 0:KW, :].astype(MXU_DTYPE)) for p in range(n_pages)]
    gate = jnp.full((N_HEADS, LANES), NEG_INF, f32)
    for n in range(nbp):
        tot = sum(jnp.sum(sa[n * pages_per_block + j], axis=1, keepdims=True) for j in range(pages_per_block))
        gate = jnp.where(lane == n, tot * (1.0 / MOBA_BLOCK), gate)
    sel = jnp.zeros((N_HEADS, LANES), f32)
    g = gate
    for _ in range(min(MOBA_TOPK, nbp + 1)):
        mx = jnp.max(g, axis=1, keepdims=True)
        first = jnp.min(jnp.where(g == mx, lane, 1 << 20), axis=1, keepdims=True)
        hit = lane == first
        sel = jnp.where(hit & (mx > 0.5 * NEG_INF), 1.0, sel)
        g = jnp.where(hit, -3e38, g)
    chosen = [jnp.max(jnp.where(lane == n, sel, 0.0), axis=1, keepdims=True) > 0.5 for n in range(nbp)]
    masks = [jnp.broadcast_to(chosen[p // pages_per_block], sa[p].shape) for p in range(n_pages)]
    oa_ref[...] = finish(sa, masks, q8, na_ref[:, 0:KW], na_ref[:, KW:2 * KW], abuf, True)

    sc_scale = (IDX_DIM ** -0.5) * (IDX_HEADS ** -0.5)
    iq8 = iq_ref[...]
    iw8 = iw_ref[...]
    sc = jnp.concatenate(
        [jnp.sum(iw8 * jnp.maximum(_dot(iq8, ibuf[slot, p].astype(MXU_DTYPE)), 0.0), axis=0, keepdims=True)
         for p in range(n_pages)], axis=1) * sc_scale
    rel_n = jnp.maximum(jnp.sum(iq8.astype(f32) * ni_ref[...], axis=1, keepdims=True), 0.0)
    sc_n = jnp.sum(iw8 * rel_n, axis=0, keepdims=True) * sc_scale
    key = _sortable(sc)
    key_n = _sortable(sc_n)
    idx = lax.broadcasted_iota(jnp.int32, (1, past), 1)
    nd = 1 << RADIX_BITS
    key_b = jnp.broadcast_to(key, (nd, past))
    idx_b = jnp.broadcast_to(idx, (nd, past))

    def count(pred_past, pred_new):
        return jnp.sum(jnp.where(pred_past, 1, 0), axis=1, keepdims=True) + jnp.where(pred_new, 1, 0)

    def enough(cand_u):
        cand = cand_u ^ jnp.int32(INT_MIN)
        return count(key_b >= cand, key_n >= cand) >= n_top

    thr = _radix_select(enough, 32, 31) ^ jnp.int32(INT_MIN)
    need = n_top - count(key > thr, key_n > thr)
    eq = key == thr
    eq_b = jnp.broadcast_to(eq, (nd, past))
    idx_bits = -(-max(1, past.bit_length()) // RADIX_BITS) * RADIX_BITS
    cut = _radix_select(lambda cand: jnp.sum(jnp.where(eq_b & (idx_b < cand), 1, 0), axis=1, keepdims=True) < need,
                        idx_bits, idx_bits - 1)
    eq_past = jnp.sum(jnp.where(eq, 1, 0), axis=1, keepdims=True)
    keep = (key > thr) | (eq & (idx <= cut))
    keep_n = (key_n > thr) | ((key_n == thr) & (eq_past < need))
    q8c = qc_ref[...]
    s_c = [_dot(q8c, cbuf[slot, p, 0:KW, :].astype(MXU_DTYPE)) for p in range(n_pages)]
    masks_c = [jnp.broadcast_to(keep[:, p * page:(p + 1) * page], s_c[p].shape) for p in range(n_pages)]
    oc_ref[...] = finish(s_c, masks_c, q8c, nc_ref[:, 0:KW], nc_ref[:, KW:2 * KW], cbuf, keep_n)


def _decode_call(page_table, cache_at, cache_ct, cache_it, layer, q8a, q8c, iq8, iw8, kva, kvc, ik):
    db, n_pages = page_table.shape
    page = cache_at.shape[3]
    past = n_pages * page
    assert past % MOBA_BLOCK == 0 and MOBA_BLOCK % page == 0 and page == LANES
    n_top = min(DSA_TOPK, (past + 1) // 4)
    assert n_top <= past
    seqspec = lambda r, w: pl.BlockSpec((None, r, w), lambda b, pt: (b, 0, 0))
    hbm = pl.BlockSpec(memory_space=pl.ANY)
    grid_spec = pltpu.PrefetchScalarGridSpec(
        num_scalar_prefetch=1,
        grid=(db,),
        in_specs=[hbm, hbm, hbm,
                  seqspec(N_HEADS, KW), seqspec(N_HEADS, KW), seqspec(IDX_HEADS, IDX_DIM), seqspec(IDX_HEADS, 1),
                  seqspec(1, 2 * KW), seqspec(1, 2 * KW), seqspec(1, IDX_DIM)],
        out_specs=[seqspec(N_HEADS, KW), seqspec(N_HEADS, KW)],
        scratch_shapes=[pltpu.VMEM((2, n_pages, 2 * KW, page), f32), pltpu.VMEM((2, n_pages, 2 * KW, page), f32),
                        pltpu.VMEM((2, n_pages, IDX_DIM, page), f32), pltpu.SemaphoreType.DMA((3, 2))],
    )
    return pl.pallas_call(
        functools.partial(_decode_body, layer=layer, n_pages=n_pages, page=page, n_top=n_top),
        grid_spec=grid_spec,
        out_shape=[jax.ShapeDtypeStruct((db, N_HEADS, KW), f32)] * 2,
        compiler_params=pltpu.CompilerParams(dimension_semantics=("arbitrary",), vmem_limit_bytes=VMEM_LIMIT),
        name="decode",
    )(page_table, cache_at, cache_ct, cache_it, q8a, q8c, iq8, iw8, kva, kvc, ik)


def _merge_body(x_ref, oa_ref, ob_ref, oc_ref, wgate_ref, wpa_ref, wpb_ref, wpc_ref, wout_ref, g_ref, b_ref, o_ref):
    x = x_ref[...]
    d = x.shape[1]
    xb = x.astype(MXU_DTYPE)
    merged = None
    for k, (o_r, wp_r) in enumerate(((oa_ref, wpa_ref), (ob_ref, wpb_ref), (oc_ref, wpc_ref))):
        gate = _sigmoid(_dot(xb, wgate_ref[:, k * d:(k + 1) * d]))
        term = gate * _dot(o_r[...].astype(MXU_DTYPE), wp_r[...])
        merged = term if merged is None else merged + term
    y = DEEPNORM_ALPHA * x + _dot(merged.astype(MXU_DTYPE), wout_ref[...])
    o_ref[...] = _layer_norm(y, g_ref[...], b_ref[...])


def _merge_call(x, oa, ob, oc, wgate, wpa, wpb, wpc, wout, g, b, *, tm):
    n, d = x.shape
    row = lambda w: pl.BlockSpec((tm, w), lambda i: (i, 0))
    return pl.pallas_call(
        _merge_body,
        grid=(n // tm,),
        in_specs=[row(d), row(BRANCH_WIDTH), row(BRANCH_WIDTH), row(BRANCH_WIDTH)]
        + [_const_spec(a.shape) for a in (wgate, wpa, wpb, wpc, wout, g, b)],
        out_specs=row(d),
        out_shape=jax.ShapeDtypeStruct((n, d), f32),
        compiler_params=pltpu.CompilerParams(dimension_semantics=("parallel",), vmem_limit_bytes=VMEM_LIMIT),
        name="merge",
    )(x, oa, ob, oc, wgate, wpa, wpb, wpc, wout, g, b)


def _moe_body(x_ref, wrh_ref, wrl_ref, br_ref, w1_ref, w3_ref, w2_ref, g_ref, b_ref, o_ref, hh_sc):
    x = x_ref[...]
    tm = x.shape[0]
    xb = x.astype(MXU_DTYPE)
    xl = (x - xb.astype(f32)).astype(MXU_DTYPE)
    logits = _dot(xb, wrh_ref[...]) + _dot(xb, wrl_ref[...]) + _dot(xl, wrh_ref[...]) + br_ref[...]
    lane = lax.broadcasted_iota(jnp.int32, (tm, LANES), 1)
    big = 1 << 20
    gl = jnp.where(lane < N_GROUPS, logits, -jnp.inf)
    gmax = jnp.max(gl, axis=1, keepdims=True)
    g_sel = jnp.min(jnp.where(gl == gmax, lane, big), axis=1, keepdims=True)
    g_w = 1.0 / jnp.sum(jnp.exp(gl - gmax), axis=1, keepdims=True)
    lo = N_GROUPS + EXPERTS_PER_GROUP * g_sel
    el = jnp.where((lane >= lo) & (lane < lo + EXPERTS_PER_GROUP), logits, -jnp.inf)
    m1 = jnp.max(el, axis=1, keepdims=True)
    i1 = jnp.min(jnp.where(el == m1, lane, big), axis=1, keepdims=True)
    el2 = jnp.where(lane == i1, -jnp.inf, el)
    m2 = jnp.max(el2, axis=1, keepdims=True)
    i2 = jnp.min(jnp.where(el2 == m2, lane, big), axis=1, keepdims=True)
    e2 = jnp.exp(m2 - m1)
    w_first = g_w / (1.0 + e2)
    w_second = g_w * e2 / (1.0 + e2)
    comb = jnp.where(lane == i1, w_first, 0.0) + jnp.where(lane == i2, w_second, 0.0)

    ff = w1_ref.shape[2]
    for e in range(N_EXPERTS):
        hg = _dot(xb, w1_ref[e])
        hu = _dot(xb, w3_ref[e])
        ce = comb[:, N_GROUPS + e:N_GROUPS + e + 1]
        hh_sc[:, e * ff:(e + 1) * ff] = (hg * _sigmoid(hg) * hu * ce).astype(hh_sc.dtype)
    y = DEEPNORM_ALPHA * x + _dot(hh_sc[...], w2_ref[...])
    o_ref[...] = _layer_norm(y, g_ref[...], b_ref[...])


def _moe_call(x, wrh, wrl, br, w1, w3, w2, g, b, *, tm):
    n, d = x.shape
    row = pl.BlockSpec((tm, d), lambda i: (i, 0))
    return pl.pallas_call(
        _moe_body,
        grid=(n // tm,),
        in_specs=[row] + [_const_spec(a.shape) for a in (wrh, wrl, br, w1, w3, w2, g, b)],
        out_specs=row,
        out_shape=jax.ShapeDtypeStruct((n, d), f32),
        scratch_shapes=[pltpu.VMEM((tm, w2.shape[0]), MXU_DTYPE)],
        compiler_params=pltpu.CompilerParams(dimension_semantics=("parallel",), vmem_limit_bytes=VMEM_LIMIT),
        name="moe",
    )(x, wrh, wrl, br, w1, w3, w2, g, b)


def _rope_table(pos, dim, ident_from=None):
    rot = dim // ROPE_FRACTION
    half = rot // 2
    inv_freq = ROPE_THETA ** (-jnp.arange(half, dtype=f32) / half)
    ang = pos.astype(f32)[:, None] * inv_freq[None, :]
    cos, sin = jnp.cos(ang), jnp.sin(ang)
    n = pos.shape[0]
    z = lambda w: jnp.zeros((n, w), f32)
    c = jnp.concatenate([cos, cos, jnp.ones((n, dim - rot), f32)], axis=1)
    sa = jnp.concatenate([z(half), sin, z(dim - rot)], axis=1)
    sb = jnp.concatenate([-sin, z(dim - half)], axis=1)
    tab = jnp.stack([jnp.tile(a, (1, LANES // dim)) for a in (c, sa, sb)])
    if ident_from is not None:
        keep = (jnp.arange(LANES) < ident_from)[None, None, :]
        ident = jnp.stack([jnp.ones((n, LANES), f32), jnp.zeros((n, LANES), f32), jnp.zeros((n, LANES), f32)])
        tab = jnp.where(keep, tab, ident)
    return tab


def _layer_params(l, w_in, w_pa, w_pb, w_pc, w_out, w_s, b_s, ln_v_g, ln_v_b, ln1_g, ln1_b, ln2_g, ln2_b,
                  w_rg, b_rg, w_re, b_re, w_e1, w_e3, w_e2):
    d = w_in.shape[1]
    w = w_in[l]
    offs = {}
    o = 0
    for name, width in (("aq", QW), ("ak", KW), ("av", KW), ("cq", QW), ("ck", KW), ("cv", KW),
                        ("iq", IQW), ("ik", IDX_DIM), ("iw", IDX_HEADS), ("gu", GMLP_WIDTH), ("gv", GMLP_WIDTH),
                        ("ga", d), ("gb", d), ("gc", d)):
        offs[name] = (o, o + width)
        o += width
    assert o == w.shape[1]
    cols = lambda n: w[:, offs[n][0]:offs[n][1]]
    pad = jnp.zeros((d, LANES - IDX_DIM - IDX_HEADS), w.dtype)
    wm = jnp.concatenate([cols("aq"), cols("cq"), cols("ak"), cols("ck"), cols("av"), cols("cv"),
                          cols("iq"), cols("ik"), cols("iw"), pad], axis=1).astype(MXU_DTYPE)
    wg = jnp.concatenate([cols("gu"), cols("gv")], axis=1).astype(MXU_DTYPE)
    wgate = jnp.concatenate([cols("ga"), cols("gb"), cols("gc")], axis=1).astype(MXU_DTYPE)
    causal = jnp.tril(jnp.ones((GMLP_CHUNK, GMLP_CHUNK), dtype=bool))
    gd = GMLP_WIDTH // GMLP_GROUPS
    ws = jnp.where(causal[None], w_s[l], 0)
    wr = jnp.concatenate([w_rg[l], w_re[l], jnp.zeros((d, LANES - N_GROUPS - N_EXPERTS), f32)], axis=1)
    wrh = wr.astype(MXU_DTYPE)
    row = lambda a: a.reshape(1, -1)
    return dict(
        wm=wm, wg=wg, wgate=wgate,
        ws_chunk=ws.astype(MXU_DTYPE),
        bs_chunk=jnp.repeat(b_s[l].T, gd, axis=1),
        ws_row=jnp.repeat(w_s[l][:, 0, 0], gd).reshape(1, -1),
        bs_row=jnp.repeat(b_s[l][:, 0], gd).reshape(1, -1),
        lnv_g=row(ln_v_g[l]), lnv_b=row(ln_v_b[l]),
        wpa=w_pa[l].astype(MXU_DTYPE), wpb=w_pb[l].astype(MXU_DTYPE), wpc=w_pc[l].astype(MXU_DTYPE),
        wout=w_out[l].astype(MXU_DTYPE), ln1_g=row(ln1_g[l]), ln1_b=row(ln1_b[l]),
        wrh=wrh, wrl=(wr - wrh.astype(f32)).astype(MXU_DTYPE),
        br=jnp.concatenate([b_rg[l], b_re[l], jnp.zeros((LANES - N_GROUPS - N_EXPERTS,), f32)]).reshape(1, -1),
        w1=w_e1[l].astype(MXU_DTYPE), w3=w_e3[l].astype(MXU_DTYPE),
        w2=w_e2[l].reshape(-1, d).astype(MXU_DTYPE), ln2_g=row(ln2_g[l]), ln2_b=row(ln2_b[l]),
    )


def kernel(x_prompt, x_sample, cache_a_kv, cache_c_kv, cache_c_kidx, page_table, w_in, w_pa, w_pb, w_pc, w_out, w_s, b_s, ln_v_g, ln_v_b, ln1_g, ln1_b, ln2_g, ln2_b, w_rg, b_rg, w_re, b_re, w_e1, w_e3, w_e2):
    bp, tp, d = x_prompt.shape
    db, ts, _ = x_sample.shape
    depth = w_in.shape[0]
    n_phys, page = cache_a_kv.shape[1], cache_a_kv.shape[2]
    n_pages = page_table.shape[1]
    past = n_pages * page
    assert ts == 1 and tp % MOBA_BLOCK == 0 and cache_a_kv.shape[3:] == (2, N_KV_HEADS, HEAD_DIM)

    cache_a = jnp.transpose(cache_a_kv, (0, 1, 3, 4, 5, 2)).reshape(depth, n_phys, 2 * KW, page)
    cache_c = jnp.transpose(cache_c_kv, (0, 1, 3, 4, 5, 2)).reshape(depth, n_phys, 2 * KW, page)
    cache_i = jnp.transpose(cache_c_kidx, (0, 1, 3, 2))
    page_table = page_table.astype(jnp.int32)

    tm_p = 512 if tp % 512 == 0 else MOBA_BLOCK
    tm_s = db
    pos_p = jnp.arange(tp, dtype=jnp.int32)
    pos_s = jnp.full((db,), past, jnp.int32)
    tabs_p = (_rope_table(pos_p, HEAD_DIM), _rope_table(pos_p, IDX_DIM), _rope_table(pos_p, IDX_DIM, IDX_DIM))
    tabs_s = (_rope_table(pos_s, HEAD_DIM), _rope_table(pos_s, IDX_DIM), _rope_table(pos_s, IDX_DIM, IDX_DIM))

    xp = x_prompt.reshape(bp * tp, d)
    xs = x_sample.reshape(db, d)
    outs = [[] for _ in range(7)]
    eye = jnp.eye(N_KV_HEADS, dtype=f32)
    for l in range(depth):
        p = _layer_params(l, w_in, w_pa, w_pb, w_pc, w_out, w_s, b_s, ln_v_g, ln_v_b, ln1_g, ln1_b, ln2_g, ln2_b,
                          w_rg, b_rg, w_re, b_re, w_e1, w_e3, w_e2)

        qa, qc, kva, kvc, kvab, kvcb, iq, ikw, ikr, ob, _ = _proj_call(
            xp, p["wm"], p["wg"], *tabs_p, p["lnv_g"], p["lnv_b"], p["ws_chunk"], p["bs_chunk"],
            tm=tm_p, chunked=True)
        oa = _moba_call(qa, kvab, batch=bp, t=tp)
        oc = _dsa_call(iq, ikw, ikr, qc, kvcb, batch=bp, t=tp)
        x1 = _merge_call(xp, oa, ob, oc, p["wgate"], p["wpa"], p["wpb"], p["wpc"], p["wout"],
                         p["ln1_g"], p["ln1_b"], tm=tm_p)
        xp = _moe_call(x1, p["wrh"], p["wrl"], p["br"], p["w1"], p["w3"], p["w2"], p["ln2_g"], p["ln2_b"], tm=tm_p)
        outs[0].append(kva.reshape(bp, tp // page, page, 2, N_KV_HEADS, HEAD_DIM))
        outs[1].append(kvc.reshape(bp, tp // page, page, 2, N_KV_HEADS, HEAD_DIM))
        outs[2].append(ikw[:, 0:IDX_DIM].reshape(bp, tp // page, page, IDX_DIM))

        qa, qc, kva, kvc, _, _, iq, ikw, _, ob, vv = _proj_call(
            xs, p["wm"], p["wg"], *tabs_s, p["lnv_g"], p["lnv_b"], p["ws_row"], p["bs_row"],
            tm=tm_s, chunked=False)
        place = lambda q: jnp.einsum("bkgd,kj->bkgjd", q.astype(f32).reshape(db, N_KV_HEADS, GROUP, HEAD_DIM),
                                     eye).reshape(db, N_HEADS, KW).astype(MXU_DTYPE)
        ik_new = ikw[:, 0:IDX_DIM]
        o8a, o8c = _decode_call(
            page_table, cache_a, cache_c, cache_i, l, place(qa), place(qc),
            iq.reshape(db, IDX_HEADS, IDX_DIM), ikw[:, IDX_DIM:IDX_DIM + IDX_HEADS].reshape(db, IDX_HEADS, 1),
            kva.reshape(db, 1, 2 * KW), kvc.reshape(db, 1, 2 * KW), ik_new.reshape(db, 1, IDX_DIM))
        unplace = lambda o: jnp.einsum("bkgkd->bkgd", o.reshape(db, N_KV_HEADS, GROUP, N_KV_HEADS, HEAD_DIM)
                                       ).reshape(db, QW).astype(MXU_DTYPE)
        x1 = _merge_call(xs, unplace(o8a), ob, unplace(o8c), p["wgate"], p["wpa"], p["wpb"], p["wpc"], p["wout"],
                         p["ln1_g"], p["ln1_b"], tm=tm_s)
        xs = _moe_call(x1, p["wrh"], p["wrl"], p["br"], p["w1"], p["w3"], p["w2"], p["ln2_g"], p["ln2_b"], tm=tm_s)
        outs[3].append(kva.reshape(db, 1, 2, N_KV_HEADS, HEAD_DIM))
        outs[4].append(kvc.reshape(db, 1, 2, N_KV_HEADS, HEAD_DIM))
        outs[5].append(ik_new.reshape(db, 1, IDX_DIM))
        outs[6].append(vv.reshape(db, 1, GMLP_WIDTH))

    return (xp.reshape(bp, tp, d), xs.reshape(db, 1, d)) + tuple(jnp.stack(o) for o in outs)
```

```python
import functools
import math

import jax
import jax.numpy as jnp
from jax import lax
from jax.experimental import pallas as pl
from jax.experimental.pallas import tpu as pltpu

HEAD_DIM = 64
N_HEADS = 8
N_KV_HEADS = 4
GROUP = N_HEADS // N_KV_HEADS
MOBA_BLOCK = 256
MOBA_TOPK = 3
IDX_HEADS = 8
IDX_DIM = 32
DSA_TOPK = 256
GMLP_CHUNK = 128
GMLP_GROUPS = 8
GMLP_WIDTH = 512
BRANCH_WIDTH = 512
N_GROUPS = 4
EXPERTS_PER_GROUP = 4
N_EXPERTS = N_GROUPS * EXPERTS_PER_GROUP
ROPE_THETA = 500000.0
ROPE_FRACTION = 4
LN_EPS = 1e-5
NEG_INF = -1e30
DEEPNORM_ALPHA = (2 * 4) ** 0.25

LANES = 128
KEY_TILE = 256
MXU_DTYPE = jnp.bfloat16
VMEM_LIMIT = 56 * 1024 * 1024

QW = N_HEADS * HEAD_DIM
KW = N_KV_HEADS * HEAD_DIM
IQW = IDX_HEADS * IDX_DIM
INT_MIN = -2147483648
HALF_NEG_KEY = -1892283083

f32 = jnp.float32


def _nt_dot(a, b):
    return lax.dot_general(a, b, (((1,), (1,)), ((), ())), preferred_element_type=f32)


def _dot(a, b):
    return jnp.dot(a, b, preferred_element_type=f32)


def _layer_norm(y, g, b):
    mu = jnp.mean(y, axis=-1, keepdims=True)
    d = y - mu
    var = jnp.mean(d * d, axis=-1, keepdims=True)
    return d * lax.rsqrt(var + LN_EPS) * g + b


def _gelu_tanh(x):
    cdf = 0.5 * (1.0 + jnp.tanh(math.sqrt(2.0 / math.pi) * (x + 0.044715 * (x * x * x))))
    return x * cdf


def _sigmoid(x):
    return 1.0 / (1.0 + jnp.exp(-x))


def _sortable(x):
    bits = pltpu.bitcast(x, jnp.int32)
    return jnp.where(bits < 0, bits ^ jnp.int32(0x7FFFFFFF), bits)


def _const_spec(shape):
    nd = len(shape)
    return pl.BlockSpec(shape, lambda *_: (0,) * nd, pipeline_mode=pl.Buffered(1))


def _rope_chunk(h, c, sa, sb, half):
    return h * c + pltpu.roll(h, half, 1) * sa + pltpu.roll(h, LANES - half, 1) * sb


def _proj_body(x_ref, wm_ref, wg_ref, t64_ref, t32_ref, tl_ref, lng_ref, lnb_ref, ws_ref, bs_ref, *out_refs, chunked):
    qa_ref, qc_ref, kva_ref, kvc_ref, iq_ref, ikw_ref, ob_ref, vv_ref = out_refs[:8]
    if chunked:
        kab_ref, kcb_ref, vta_ref, vtc_ref, ikr_ref = out_refs[8:]
    xb = x_ref[...].astype(MXU_DTYPE)
    tm = xb.shape[0]
    c64, sa64, sb64 = t64_ref[0], t64_ref[1], t64_ref[2]
    c32, sa32, sb32 = t32_ref[0], t32_ref[1], t32_ref[2]
    cl, sal, sbl = tl_ref[0], tl_ref[1], tl_ref[2]
    h64 = (HEAD_DIM // ROPE_FRACTION) // 2
    h32 = (IDX_DIM // ROPE_FRACTION) // 2
    scale = HEAD_DIM ** -0.5 * math.log2(math.e)

    hq = _dot(xb, wm_ref[:, 0:2 * QW])
    for c in range(2 * QW // LANES):
        r = _rope_chunk(hq[:, c * LANES:(c + 1) * LANES], c64, sa64, sb64, h64) * scale
        dst = qa_ref if c < QW // LANES else qc_ref
        cc = c % (QW // LANES)
        dst[:, cc * LANES:(cc + 1) * LANES] = r.astype(dst.dtype)

    o = 2 * QW
    hk = _dot(xb, wm_ref[:, o:o + 2 * KW])
    hv = _dot(xb, wm_ref[:, o + 2 * KW:o + 4 * KW])
    for c in range(2 * KW // LANES):
        r = _rope_chunk(hk[:, c * LANES:(c + 1) * LANES], c64, sa64, sb64, h64)
        v = hv[:, c * LANES:(c + 1) * LANES]
        first = c < KW // LANES
        dst = kva_ref if first else kvc_ref
        cc = c % (KW // LANES)
        dst[:, cc * LANES:(cc + 1) * LANES] = r
        dst[:, KW + cc * LANES:KW + (cc + 1) * LANES] = v
        if chunked:
            kb_ref, vt_ref = (kab_ref, vta_ref) if first else (kcb_ref, vtc_ref)
            kb_ref[:, cc * LANES:(cc + 1) * LANES] = r.astype(kb_ref.dtype)
            for j in range(tm // KEY_TILE):
                vt_ref[j, cc * LANES:(cc + 1) * LANES, :] = v[j * KEY_TILE:(j + 1) * KEY_TILE, :].T.astype(vt_ref.dtype)

    o = 2 * QW + 4 * KW
    hi = _dot(xb, wm_ref[:, o:o + IQW + LANES])
    for c in range(IQW // LANES):
        r = _rope_chunk(hi[:, c * LANES:(c + 1) * LANES], c32, sa32, sb32, h32)
        iq_ref[:, c * LANES:(c + 1) * LANES] = r.astype(iq_ref.dtype)
    last = _rope_chunk(hi[:, IQW:IQW + LANES], cl, sal, sbl, h32)
    ikw_ref[...] = last
    if chunked:
        lane = lax.broadcasted_iota(jnp.int32, (tm, LANES), 1)
        t = jnp.where(lane < IDX_DIM, last, 0.0)
        t = t + pltpu.roll(t, IDX_DIM, 1)
        t = t + pltpu.roll(t, 2 * IDX_DIM, 1)
        for c in range(IQW // LANES):
            ikr_ref[:, c * LANES:(c + 1) * LANES] = t.astype(ikr_ref.dtype)

    hg = _dot(xb, wg_ref[...])
    u = _gelu_tanh(hg[:, 0:GMLP_WIDTH])
    vv = _layer_norm(_gelu_tanh(hg[:, GMLP_WIDTH:2 * GMLP_WIDTH]), lng_ref[...], lnb_ref[...])
    vv_ref[...] = vv
    if chunked:
        vb = vv.astype(MXU_DTYPE)
        gd = GMLP_WIDTH // GMLP_GROUPS
        lane_c = lax.broadcasted_iota(jnp.int32, (GMLP_CHUNK, LANES), 1)
        for c in range(tm // GMLP_CHUNK):
            rows = slice(c * GMLP_CHUNK, (c + 1) * GMLP_CHUNK)
            for pr in range(GMLP_WIDTH // LANES):
                vp = vb[rows, pr * LANES:(pr + 1) * LANES]
                g0 = (pr * LANES) // gd
                r0 = _dot(ws_ref[g0], vp)
                r1 = _dot(ws_ref[g0 + 1], vp)
                s = jnp.where(lane_c < gd, r0, r1) + bs_ref[:, pr * LANES:(pr + 1) * LANES]
                ob_ref[rows, pr * LANES:(pr + 1) * LANES] = (u[rows, pr * LANES:(pr + 1) * LANES] * s).astype(ob_ref.dtype)
    else:
        ob_ref[...] = (u * (vv * ws_ref[...] + bs_ref[...])).astype(ob_ref.dtype)


def _proj_call(x, wm, wg, t64, t32, tl, lng, lnb, ws, bs, *, tm, chunked):
    n, d = x.shape
    nt = t64.shape[1] // tm
    grid = (n // tm,)
    row = lambda w: pl.BlockSpec((tm, w), lambda i: (i, 0))
    tab = pl.BlockSpec((3, tm, LANES), lambda i: (0, i % nt, 0))
    outs = [
        (QW, MXU_DTYPE), (QW, MXU_DTYPE),
        (2 * KW, f32), (2 * KW, f32),
        (IQW, MXU_DTYPE),
        (LANES, f32),
        (GMLP_WIDTH, MXU_DTYPE),
        (GMLP_WIDTH, f32),
    ]
    out_specs = [row(w) for w, _ in outs]
    out_shape = [jax.ShapeDtypeStruct((n, w), dt) for w, dt in outs]
    if chunked:
        assert tm % KEY_TILE == 0
        vt_spec = pl.BlockSpec((tm // KEY_TILE, KW, KEY_TILE), lambda i: (i, 0, 0))
        vt_shape = jax.ShapeDtypeStruct((n // KEY_TILE, KW, KEY_TILE), MXU_DTYPE)
        out_specs += [row(KW), row(KW), vt_spec, vt_spec, row(IQW)]
        out_shape += [jax.ShapeDtypeStruct((n, KW), MXU_DTYPE)] * 2 + [vt_shape] * 2 \
            + [jax.ShapeDtypeStruct((n, IQW), MXU_DTYPE)]
    return pl.pallas_call(
        functools.partial(_proj_body, chunked=chunked),
        grid=grid,
        in_specs=[row(d), _const_spec(wm.shape), _const_spec(wg.shape), tab, tab, tab,
                  _const_spec(lng.shape), _const_spec(lnb.shape), _const_spec(ws.shape), _const_spec(bs.shape)],
        out_specs=out_specs,
        out_shape=out_shape,
        compiler_params=pltpu.CompilerParams(dimension_semantics=("parallel",), vmem_limit_bytes=VMEM_LIMIT),
        name="proj",
    )(x, wm, wg, t64, t32, tl, lng, lnb, ws, bs)


def _head_rows(q_ref, kvh, tq):
    qp = q_ref[:, kvh * LANES:(kvh + 1) * LANES].astype(f32)
    rl = pltpu.roll(qp, HEAD_DIM, 1)
    lo = lax.broadcasted_iota(jnp.int32, (tq, LANES), 1) < HEAD_DIM
    if kvh % 2 == 0:
        g0, g1 = jnp.where(lo, qp, 0.0), jnp.where(lo, rl, 0.0)
    else:
        g0, g1 = jnp.where(lo, 0.0, rl), jnp.where(lo, 0.0, qp)
    return jnp.concatenate([g0, g1], axis=0).astype(MXU_DTYPE)


ACC_ROWS = HEAD_DIM + 16


def _value_rows(vt_ref, kt, kvh):
    vt = vt_ref[kt, kvh * HEAD_DIM:(kvh + 1) * HEAD_DIM, :]
    return jnp.concatenate([vt, jnp.ones((ACC_ROWS - HEAD_DIM, vt.shape[1]), vt.dtype)], axis=0)


def _attend_tile(k_ref, vt_ref, qh, acc_sc, state, kt, masks):
    start = pl.multiple_of(kt * KEY_TILE, KEY_TILE)
    out = []
    for kvh in range(N_KV_HEADS):
        pc = kvh // 2
        s = _nt_dot(k_ref[pl.ds(start, KEY_TILE), pc * LANES:(pc + 1) * LANES], qh[kvh])
        m, acc = _softmax_step(s, masks[kvh], state[kvh], acc_sc[kvh], _value_rows(vt_ref, kt, kvh))
        acc_sc[kvh] = acc
        out.append(m)
    return tuple(out)


def _head_out(acc, tq):
    o = acc[0:HEAD_DIM] / acc[HEAD_DIM:HEAD_DIM + 1]
    return jnp.concatenate([o[:, g * tq:(g + 1) * tq] for g in range(GROUP)], axis=0).T


def _softmax_step(s, mask, m, acc, vt):
    s = jnp.where(mask, s, NEG_INF)
    m_new = jnp.maximum(m, jnp.max(s, axis=0, keepdims=True))
    p = jnp.exp2(s - m_new)
    acc_new = jnp.exp2(m - m_new) * acc + _dot(vt, p.astype(MXU_DTYPE))
    return m_new, acc_new


def _moba_body(q_ref, k_ref, vt_ref, o_ref, kmean_sc, sel_sc, acc_sc, *, tq):
    i = pl.program_id(1)
    nb = k_ref.shape[0] // MOBA_BLOCK
    nbp = kmean_sc.shape[0]
    cols = GROUP * tq

    @pl.when(i == 0)
    def _():
        kmean_sc[...] = jnp.zeros_like(kmean_sc)
        for n in range(nb):
            blk = k_ref[n * MOBA_BLOCK:(n + 1) * MOBA_BLOCK, :].astype(f32)
            kmean_sc[n:n + 1, :] = jnp.mean(blk, axis=0, keepdims=True)

    blk_id = lax.broadcasted_iota(jnp.int32, (nbp, cols), 0)
    qh = [_head_rows(q_ref, kvh, tq) for kvh in range(N_KV_HEADS)]

    for kvh in range(N_KV_HEADS):
        pc = kvh // 2
        gate = _nt_dot(kmean_sc[:, pc * LANES:(pc + 1) * LANES].astype(MXU_DTYPE), qh[kvh])
        g = jnp.where(blk_id < i, gate, NEG_INF)
        sel = jnp.zeros((nbp, cols), f32)
        for _ in range(MOBA_TOPK):
            mx = jnp.max(g, axis=0, keepdims=True)
            first = jnp.min(jnp.where(g == mx, blk_id, 1 << 20), axis=0, keepdims=True)
            hit = blk_id == first
            sel = jnp.where(hit & (mx > 0.5 * NEG_INF), 1.0, sel)
            g = jnp.where(hit, -3e38, g)
        sel_sc[kvh] = sel

    acc_sc[...] = jnp.zeros_like(acc_sc)
    key_in_blk = lax.broadcasted_iota(jnp.int32, (KEY_TILE, cols), 0)
    q_in_blk = lax.broadcasted_iota(jnp.int32, (KEY_TILE, cols), 1) % tq
    state = tuple(jnp.full((1, cols), NEG_INF, f32) for _ in range(N_KV_HEADS))
    state = _attend_tile(k_ref, vt_ref, qh, acc_sc, state, i, [key_in_blk <= q_in_blk] * N_KV_HEADS)

    def past(n, state):
        chosen = [jnp.broadcast_to(sel_sc[kvh, pl.ds(n, 1), :] > 0.5, (KEY_TILE, cols)) for kvh in range(N_KV_HEADS)]
        return _attend_tile(k_ref, vt_ref, qh, acc_sc, state, n, chosen)

    lax.fori_loop(0, i, past, state)
    for kvh in range(N_KV_HEADS):
        o_ref[:, kvh * LANES:(kvh + 1) * LANES] = _head_out(acc_sc[kvh], tq).astype(o_ref.dtype)


def _moba_call(q, kb, vt, *, batch, t):
    tq = MOBA_BLOCK
    assert tq == KEY_TILE and GROUP * HEAD_DIM == LANES
    nq = t // tq
    nbp = -(-nq // 16) * 16
    return pl.pallas_call(
        functools.partial(_moba_body, tq=tq),
        grid=(batch, nq),
        in_specs=[pl.BlockSpec((tq, QW), lambda b, i: (b * nq + i, 0)),
                  pl.BlockSpec((t, KW), lambda b, i: (b, 0)),
                  pl.BlockSpec((nq, KW, KEY_TILE), lambda b, i: (b, 0, 0))],
        out_specs=pl.BlockSpec((tq, QW), lambda b, i: (b * nq + i, 0)),
        out_shape=jax.ShapeDtypeStruct(q.shape, MXU_DTYPE),
        scratch_shapes=[pltpu.VMEM((nbp, KW), f32), pltpu.VMEM((N_KV_HEADS, nbp, GROUP * tq), f32),
                        pltpu.VMEM((N_KV_HEADS, ACC_ROWS, GROUP * tq), f32)],
        compiler_params=pltpu.CompilerParams(dimension_semantics=("parallel", "arbitrary"),
                                             vmem_limit_bytes=VMEM_LIMIT),
        name="moba",
    )(q, kb, vt)


def _select_threshold(count_ge, n_top, rows):
    def step(it, t_u):
        cand_u = t_u | lax.shift_left(jnp.int32(1), jnp.int32(31) - it)
        cnt = count_ge(cand_u ^ jnp.int32(INT_MIN))
        return jnp.where(cnt >= n_top, cand_u, t_u)
    t_u = lax.fori_loop(0, 32, step, jnp.zeros((1, rows), jnp.int32))
    return t_u ^ jnp.int32(INT_MIN)


def _dsa_body(iq_ref, ikw_ref, ikr_ref, q_ref, k_ref, vt_ref, o_ref, key_sc, cut_sc, acc_sc, *, tq, n_top, idx_bits):
    i = pl.program_id(1)
    ntile = i + 1
    cols = GROUP * tq
    lane_q = lax.broadcasted_iota(jnp.int32, (tq, IQW), 1)
    krow = lax.broadcasted_iota(jnp.int32, (tq, tq), 0)
    qcol = lax.broadcasted_iota(jnp.int32, (tq, tq), 1)
    half_neg = jnp.int32(HALF_NEG_KEY)

    iqv = iq_ref[...]
    iqm = [jnp.where((lane_q >= h * IDX_DIM) & (lane_q < (h + 1) * IDX_DIM), iqv, jnp.zeros_like(iqv))
           for h in range(IDX_HEADS)]
    iwt = ikw_ref[...].T
    iw = [iwt[IDX_DIM + h:IDX_DIM + h + 1, :] for h in range(IDX_HEADS)]
    sc_scale = (IDX_DIM ** -0.5) * (IDX_HEADS ** -0.5)

    def score_tile(kt, _):
        ik_t = ikr_ref[pl.ds(pl.multiple_of(kt * tq, tq), tq), :]
        sc = jnp.zeros((tq, tq), f32)
        for h in range(IDX_HEADS):
            sc = sc + iw[h] * jnp.maximum(_nt_dot(ik_t, iqm[h]), 0.0)
        sc = sc * sc_scale
        sc = jnp.where(kt * tq + krow <= i * tq + qcol, sc, NEG_INF)
        key_sc[kt] = _sortable(sc)
        return 0

    lax.fori_loop(0, ntile, score_tile, 0)

    def count_where(pred):
        def body(kt, c8):
            hit = jnp.where(pred(key_sc[kt], kt), 1, 0)
            return c8 + jnp.sum(hit.reshape(tq // 8, 8, tq), axis=0)
        c8 = lax.fori_loop(0, ntile, body, jnp.zeros((8, tq), jnp.int32))
        return jnp.sum(c8, axis=0, keepdims=True)

    thr = _select_threshold(lambda cand: count_where(lambda k, kt: k >= cand), n_top, tq)
    cnt_gt = count_where(lambda k, kt: k > thr)
    cnt_eq = count_where(lambda k, kt: k == thr)
    need = n_top - cnt_gt

    cut_sc[...] = jnp.full(cut_sc.shape, 1 << 30, jnp.int32)
    excess = (cnt_eq > need) & (thr > half_neg)

    @pl.when(jnp.max(jnp.where(excess, 1, 0)) > 0)
    def _():
        def step(it, t_i):
            cand = t_i | lax.shift_left(jnp.int32(1), jnp.int32(idx_bits - 1) - it)
            cnt = count_where(lambda k, kt: (k == thr) & (kt * tq + krow < cand))
            return jnp.where(cnt < need, cand, t_i)
        t_i = lax.fori_loop(0, idx_bits, step, jnp.zeros((1, tq), jnp.int32))
        cut_sc[...] = jnp.broadcast_to(jnp.where(excess, t_i, 1 << 30), cut_sc.shape)

    cut = cut_sc[0:1, :]

    qh = [_head_rows(q_ref, kvh, tq) for kvh in range(N_KV_HEADS)]
    acc_sc[...] = jnp.zeros_like(acc_sc)

    def attend(kt, state):
        k = key_sc[kt]
        keep = ((k > thr) | ((k == thr) & (kt * tq + krow <= cut))) & (k > half_neg)
        mask = jnp.concatenate([keep] * GROUP, axis=1)
        return _attend_tile(k_ref, vt_ref, qh, acc_sc, state, kt, [mask] * N_KV_HEADS)

    lax.fori_loop(0, ntile, attend, tuple(jnp.full((1, cols), NEG_INF, f32) for _ in range(N_KV_HEADS)))
    for kvh in range(N_KV_HEADS):
        o_ref[:, kvh * LANES:(kvh + 1) * LANES] = _head_out(acc_sc[kvh], tq).astype(o_ref.dtype)


def _dsa_call(iq, ikw, ikr, q, kb, vt, *, batch, t):
    tq = KEY_TILE
    nq = t // tq
    n_top = min(DSA_TOPK, t // 4)
    assert n_top <= tq and t % tq == 0 and GROUP * HEAD_DIM == LANES
    rowspec = lambda w: pl.BlockSpec((tq, w), lambda b, i: (b * nq + i, 0))
    seqspec = lambda w: pl.BlockSpec((t, w), lambda b, i: (b, 0))
    return pl.pallas_call(
        functools.partial(_dsa_body, tq=tq, n_top=n_top, idx_bits=max(1, (t - 1).bit_length())),
        grid=(batch, nq),
        in_specs=[rowspec(IQW), rowspec(LANES), seqspec(IQW), rowspec(QW), seqspec(KW),
                  pl.BlockSpec((nq, KW, KEY_TILE), lambda b, i: (b, 0, 0))],
        out_specs=rowspec(QW),
        out_shape=jax.ShapeDtypeStruct(q.shape, MXU_DTYPE),
        scratch_shapes=[pltpu.VMEM((nq, tq, tq), jnp.int32), pltpu.VMEM((8, tq), jnp.int32),
                        pltpu.VMEM((N_KV_HEADS, ACC_ROWS, GROUP * tq), f32)],
        compiler_params=pltpu.CompilerParams(dimension_semantics=("parallel", "arbitrary"),
                                             vmem_limit_bytes=VMEM_LIMIT),
        name="dsa",
    )(iq, ikw, ikr, q, kb, vt)


RADIX_BITS = 4


def _radix_select(count_fn, total_bits, top_bit):
    digit = lax.broadcasted_iota(jnp.int32, (1 << RADIX_BITS, 1), 0)
    t = jnp.zeros((1, 1), jnp.int32)
    for step in range(total_bits // RADIX_BITS):
        shift = top_bit + 1 - RADIX_BITS * (step + 1)
        ok = count_fn(t | lax.shift_left(digit, jnp.int32(shift)))
        best = jnp.sum(jnp.where(ok, 1, 0), axis=0, keepdims=True) - 1
        t = t | lax.shift_left(best, jnp.int32(shift))
    return t


def _decode_body(pt_ref, ca_hbm, cc_hbm, ci_hbm, qa_ref, qc_ref, iq_ref, iw_ref, na_ref, nc_ref, ni_ref,
                 oa_ref, oc_ref, abuf, cbuf, ibuf, sems, *, layer, n_pages, page, n_top):
    b = pl.program_id(0)
    slot = lax.rem(b, 2)
    past = n_pages * page
    pages_per_block = MOBA_BLOCK // page
    nbp = n_pages // pages_per_block

    def page_copies(seq, sl):
        out = []
        for p in range(n_pages):
            pg = pt_ref[seq, p]
            out.append(pltpu.make_async_copy(ca_hbm.at[layer, pg], abuf.at[sl, p], sems.at[0, sl]))
            out.append(pltpu.make_async_copy(cc_hbm.at[layer, pg], cbuf.at[sl, p], sems.at[1, sl]))
            out.append(pltpu.make_async_copy(ci_hbm.at[layer, pg], ibuf.at[sl, p], sems.at[2, sl]))
        return out

    @pl.when(b == 0)
    def _():
        for c in page_copies(0, 0):
            c.start()

    @pl.when(b + 1 < pl.num_programs(0))
    def _():
        for c in page_copies(b + 1, 1 - slot):
            c.start()

    for c in page_copies(b, slot):
        c.wait()

    lane = lax.broadcasted_iota(jnp.int32, (N_HEADS, LANES), 1)

    def finish(s_list, mask_list, q8, knew, vnew, vbuf, extra_ok):
        s_new = jnp.sum(q8.astype(f32) * knew, axis=1, keepdims=True)
        s_new = jnp.where(extra_ok, s_new, NEG_INF)
        sm = [jnp.where(mk, s, NEG_INF) for s, mk in zip(s_list, mask_list)]
        m = s_new
        for s in sm:
            m = jnp.maximum(m, jnp.max(s, axis=1, keepdims=True))
        p_new = jnp.where(extra_ok, jnp.exp2(s_new - m), 0.0)
        l = p_new
        acc = p_new * vnew
        for p, (s, mk) in enumerate(zip(sm, mask_list)):
            pc = jnp.where(mk, jnp.exp2(s - m), 0.0)
            l = l + jnp.sum(pc, axis=1, keepdims=True)
            acc = acc + _nt_dot(pc.astype(MXU_DTYPE), vbuf[slot, p, KW:2 * KW, :].astype(MXU_DTYPE))
        return acc / l

    q8 = qa_ref[...]
    sa = [_dot(q8, abuf[slot, p, 0:KW, :].astype(MXU_DTYPE)) for p in range(n_pages)]
    gate = jnp.full((N_HEADS, LANES), NEG_INF, f32)
    for n in range(nbp):
        tot = sum(jnp.sum(sa[n * pages_per_block + j], axis=1, keepdims=True) for j in range(pages_per_block))
        gate = jnp.where(lane == n, tot * (1.0 / MOBA_BLOCK), gate)
    sel = jnp.zeros((N_HEADS, LANES), f32)
    g = gate
    for _ in range(min(MOBA_TOPK, nbp + 1)):
        mx = jnp.max(g, axis=1, keepdims=True)
        first = jnp.min(jnp.where(g == mx, lane, 1 << 20), axis=1, keepdims=True)
        hit = lane == first
        sel = jnp.where(hit & (mx > 0.5 * NEG_INF), 1.0, sel)
        g = jnp.where(hit, -3e38, g)
    chosen = [jnp.max(jnp.where(lane == n, sel, 0.0), axis=1, keepdims=True) > 0.5 for n in range(nbp)]
    masks = [jnp.broadcast_to(chosen[p // pages_per_block], sa[p].shape) for p in range(n_pages)]
    oa_ref[...] = finish(sa, masks, q8, na_ref[:, 0:KW], na_ref[:, KW:2 * KW], abuf, True)

    sc_scale = (IDX_DIM ** -0.5) * (IDX_HEADS ** -0.5)
    iq8 = iq_ref[...]
    iw8 = iw_ref[...]
    sc = jnp.concatenate(
        [jnp.sum(iw8 * jnp.maximum(_dot(iq8, ibuf[slot, p].astype(MXU_DTYPE)), 0.0), axis=0, keepdims=True)
         for p in range(n_pages)], axis=1) * sc_scale
    rel_n = jnp.maximum(jnp.sum(iq8.astype(f32) * ni_ref[...], axis=1, keepdims=True), 0.0)
    sc_n = jnp.sum(iw8 * rel_n, axis=0, keepdims=True) * sc_scale
    key = _sortable(sc)
    key_n = _sortable(sc_n)
    idx = lax.broadcasted_iota(jnp.int32, (1, past), 1)
    nd = 1 << RADIX_BITS
    key_b = jnp.broadcast_to(key, (nd, past))
    idx_b = jnp.broadcast_to(idx, (nd, past))

    def count(pred_past, pred_new):
        return jnp.sum(jnp.where(pred_past, 1, 0), axis=1, keepdims=True) + jnp.where(pred_new, 1, 0)

    def enough(cand_u):
        cand = cand_u ^ jnp.int32(INT_MIN)
        return count(key_b >= cand, key_n >= cand) >= n_top

    thr = _radix_select(enough, 32, 31) ^ jnp.int32(INT_MIN)
    need = n_top - count(key > thr, key_n > thr)
    eq = key == thr
    eq_b = jnp.broadcast_to(eq, (nd, past))
    idx_bits = -(-max(1, past.bit_length()) // RADIX_BITS) * RADIX_BITS
    cut = _radix_select(lambda cand: jnp.sum(jnp.where(eq_b & (idx_b < cand), 1, 0), axis=1, keepdims=True) < need,
                        idx_bits, idx_bits - 1)
    eq_past = jnp.sum(jnp.where(eq, 1, 0), axis=1, keepdims=True)
    keep = (key > thr) | (eq & (idx <= cut))
    keep_n = (key_n > thr) | ((key_n == thr) & (eq_past < need))
    q8c = qc_ref[...]
    s_c = [_dot(q8c, cbuf[slot, p, 0:KW, :].astype(MXU_DTYPE)) for p in range(n_pages)]
    masks_c = [jnp.broadcast_to(keep[:, p * page:(p + 1) * page], s_c[p].shape) for p in range(n_pages)]
    oc_ref[...] = finish(s_c, masks_c, q8c, nc_ref[:, 0:KW], nc_ref[:, KW:2 * KW], cbuf, keep_n)


def _decode_call(page_table, cache_at, cache_ct, cache_it, layer, q8a, q8c, iq8, iw8, kva, kvc, ik):
    db, n_pages = page_table.shape
    page = cache_at.shape[3]
    past = n_pages * page
    assert past % MOBA_BLOCK == 0 and MOBA_BLOCK % page == 0 and page == LANES
    n_top = min(DSA_TOPK, (past + 1) // 4)
    assert n_top <= past
    seqspec = lambda r, w: pl.BlockSpec((None, r, w), lambda b, pt: (b, 0, 0))
    hbm = pl.BlockSpec(memory_space=pl.ANY)
    grid_spec = pltpu.PrefetchScalarGridSpec(
        num_scalar_prefetch=1,
        grid=(db,),
        in_specs=[hbm, hbm, hbm,
                  seqspec(N_HEADS, KW), seqspec(N_HEADS, KW), seqspec(IDX_HEADS, IDX_DIM), seqspec(IDX_HEADS, 1),
                  seqspec(1, 2 * KW), seqspec(1, 2 * KW), seqspec(1, IDX_DIM)],
        out_specs=[seqspec(N_HEADS, KW), seqspec(N_HEADS, KW)],
        scratch_shapes=[pltpu.VMEM((2, n_pages, 2 * KW, page), f32), pltpu.VMEM((2, n_pages, 2 * KW, page), f32),
                        pltpu.VMEM((2, n_pages, IDX_DIM, page), f32), pltpu.SemaphoreType.DMA((3, 2))],
    )
    return pl.pallas_call(
        functools.partial(_decode_body, layer=layer, n_pages=n_pages, page=page, n_top=n_top),
        grid_spec=grid_spec,
        out_shape=[jax.ShapeDtypeStruct((db, N_HEADS, KW), f32)] * 2,
        compiler_params=pltpu.CompilerParams(dimension_semantics=("arbitrary",), vmem_limit_bytes=VMEM_LIMIT),
        name="decode",
    )(page_table, cache_at, cache_ct, cache_it, q8a, q8c, iq8, iw8, kva, kvc, ik)


def _merge_body(x_ref, oa_ref, ob_ref, oc_ref, wgate_ref, wpa_ref, wpb_ref, wpc_ref, wout_ref, g_ref, b_ref, o_ref):
    x = x_ref[...]
    d = x.shape[1]
    xb = x.astype(MXU_DTYPE)
    merged = None
    for k, (o_r, wp_r) in enumerate(((oa_ref, wpa_ref), (ob_ref, wpb_ref), (oc_ref, wpc_ref))):
        gate = _sigmoid(_dot(xb, wgate_ref[:, k * d:(k + 1) * d]))
        term = gate * _dot(o_r[...].astype(MXU_DTYPE), wp_r[...])
        merged = term if merged is None else merged + term
    y = DEEPNORM_ALPHA * x + _dot(merged.astype(MXU_DTYPE), wout_ref[...])
    o_ref[...] = _layer_norm(y, g_ref[...], b_ref[...])


def _merge_call(x, oa, ob, oc, wgate, wpa, wpb, wpc, wout, g, b, *, tm):
    n, d = x.shape
    row = lambda w: pl.BlockSpec((tm, w), lambda i: (i, 0))
    return pl.pallas_call(
        _merge_body,
        grid=(n // tm,),
        in_specs=[row(d), row(BRANCH_WIDTH), row(BRANCH_WIDTH), row(BRANCH_WIDTH)]
        + [_const_spec(a.shape) for a in (wgate, wpa, wpb, wpc, wout, g, b)],
        out_specs=row(d),
        out_shape=jax.ShapeDtypeStruct((n, d), f32),
        compiler_params=pltpu.CompilerParams(dimension_semantics=("parallel",), vmem_limit_bytes=VMEM_LIMIT),
        name="merge",
    )(x, oa, ob, oc, wgate, wpa, wpb, wpc, wout, g, b)


def _moe_body(x_ref, wrh_ref, wrl_ref, br_ref, w1_ref, w3_ref, w2_ref, g_ref, b_ref, o_ref, hh_sc):
    x = x_ref[...]
    tm = x.shape[0]
    xb = x.astype(MXU_DTYPE)
    xl = (x - xb.astype(f32)).astype(MXU_DTYPE)
    logits = _dot(xb, wrh_ref[...]) + _dot(xb, wrl_ref[...]) + _dot(xl, wrh_ref[...]) + br_ref[...]
    lane = lax.broadcasted_iota(jnp.int32, (tm, LANES), 1)
    big = 1 << 20
    gl = jnp.where(lane < N_GROUPS, logits, -jnp.inf)
    gmax = jnp.max(gl, axis=1, keepdims=True)
    g_sel = jnp.min(jnp.where(gl == gmax, lane, big), axis=1, keepdims=True)
    g_w = 1.0 / jnp.sum(jnp.exp(gl - gmax), axis=1, keepdims=True)
    lo = N_GROUPS + EXPERTS_PER_GROUP * g_sel
    el = jnp.where((lane >= lo) & (lane < lo + EXPERTS_PER_GROUP), logits, -jnp.inf)
    m1 = jnp.max(el, axis=1, keepdims=True)
    i1 = jnp.min(jnp.where(el == m1, lane, big), axis=1, keepdims=True)
    el2 = jnp.where(lane == i1, -jnp.inf, el)
    m2 = jnp.max(el2, axis=1, keepdims=True)
    i2 = jnp.min(jnp.where(el2 == m2, lane, big), axis=1, keepdims=True)
    e2 = jnp.exp(m2 - m1)
    w_first = g_w / (1.0 + e2)
    w_second = g_w * e2 / (1.0 + e2)
    comb = jnp.where(lane == i1, w_first, 0.0) + jnp.where(lane == i2, w_second, 0.0)

    ff = w1_ref.shape[2]
    for e in range(N_EXPERTS):
        hg = _dot(xb, w1_ref[e])
        hu = _dot(xb, w3_ref[e])
        ce = comb[:, N_GROUPS + e:N_GROUPS + e + 1]
        hh_sc[:, e * ff:(e + 1) * ff] = (hg * _sigmoid(hg) * hu * ce).astype(hh_sc.dtype)
    y = DEEPNORM_ALPHA * x + _dot(hh_sc[...], w2_ref[...])
    o_ref[...] = _layer_norm(y, g_ref[...], b_ref[...])


def _moe_call(x, wrh, wrl, br, w1, w3, w2, g, b, *, tm):
    n, d = x.shape
    row = pl.BlockSpec((tm, d), lambda i: (i, 0))
    return pl.pallas_call(
        _moe_body,
        grid=(n // tm,),
        in_specs=[row] + [_const_spec(a.shape) for a in (wrh, wrl, br, w1, w3, w2, g, b)],
        out_specs=row,
        out_shape=jax.ShapeDtypeStruct((n, d), f32),
        scratch_shapes=[pltpu.VMEM((tm, w2.shape[0]), MXU_DTYPE)],
        compiler_params=pltpu.CompilerParams(dimension_semantics=("parallel",), vmem_limit_bytes=VMEM_LIMIT),
        name="moe",
    )(x, wrh, wrl, br, w1, w3, w2, g, b)


def _rope_table(pos, dim, ident_from=None):
    rot = dim // ROPE_FRACTION
    half = rot // 2
    inv_freq = ROPE_THETA ** (-jnp.arange(half, dtype=f32) / half)
    ang = pos.astype(f32)[:, None] * inv_freq[None, :]
    cos, sin = jnp.cos(ang), jnp.sin(ang)
    n = pos.shape[0]
    z = lambda w: jnp.zeros((n, w), f32)
    c = jnp.concatenate([cos, cos, jnp.ones((n, dim - rot), f32)], axis=1)
    sa = jnp.concatenate([z(half), sin, z(dim - rot)], axis=1)
    sb = jnp.concatenate([-sin, z(dim - half)], axis=1)
    tab = jnp.stack([jnp.tile(a, (1, LANES // dim)) for a in (c, sa, sb)])
    if ident_from is not None:
        keep = (jnp.arange(LANES) < ident_from)[None, None, :]
        ident = jnp.stack([jnp.ones((n, LANES), f32), jnp.zeros((n, LANES), f32), jnp.zeros((n, LANES), f32)])
        tab = jnp.where(keep, tab, ident)
    return tab


def _layer_params(l, w_in, w_pa, w_pb, w_pc, w_out, w_s, b_s, ln_v_g, ln_v_b, ln1_g, ln1_b, ln2_g, ln2_b,
                  w_rg, b_rg, w_re, b_re, w_e1, w_e3, w_e2):
    d = w_in.shape[1]
    w = w_in[l]
    offs = {}
    o = 0
    for name, width in (("aq", QW), ("ak", KW), ("av", KW), ("cq", QW), ("ck", KW), ("cv", KW),
                        ("iq", IQW), ("ik", IDX_DIM), ("iw", IDX_HEADS), ("gu", GMLP_WIDTH), ("gv", GMLP_WIDTH),
                        ("ga", d), ("gb", d), ("gc", d)):
        offs[name] = (o, o + width)
        o += width
    assert o == w.shape[1]
    cols = lambda n: w[:, offs[n][0]:offs[n][1]]
    pad = jnp.zeros((d, LANES - IDX_DIM - IDX_HEADS), w.dtype)
    wm = jnp.concatenate([cols("aq"), cols("cq"), cols("ak"), cols("ck"), cols("av"), cols("cv"),
                          cols("iq"), cols("ik"), cols("iw"), pad], axis=1).astype(MXU_DTYPE)
    wg = jnp.concatenate([cols("gu"), cols("gv")], axis=1).astype(MXU_DTYPE)
    wgate = jnp.concatenate([cols("ga"), cols("gb"), cols("gc")], axis=1).astype(MXU_DTYPE)
    causal = jnp.tril(jnp.ones((GMLP_CHUNK, GMLP_CHUNK), dtype=bool))
    gd = GMLP_WIDTH // GMLP_GROUPS
    ws = jnp.where(causal[None], w_s[l], 0)
    wr = jnp.concatenate([w_rg[l], w_re[l], jnp.zeros((d, LANES - N_GROUPS - N_EXPERTS), f32)], axis=1)
    wrh = wr.astype(MXU_DTYPE)
    row = lambda a: a.reshape(1, -1)
    return dict(
        wm=wm, wg=wg, wgate=wgate,
        ws_chunk=ws.astype(MXU_DTYPE),
        bs_chunk=jnp.repeat(b_s[l].T, gd, axis=1),
        ws_row=jnp.repeat(w_s[l][:, 0, 0], gd).reshape(1, -1),
        bs_row=jnp.repeat(b_s[l][:, 0], gd).reshape(1, -1),
        lnv_g=row(ln_v_g[l]), lnv_b=row(ln_v_b[l]),
        wpa=w_pa[l].astype(MXU_DTYPE), wpb=w_pb[l].astype(MXU_DTYPE), wpc=w_pc[l].astype(MXU_DTYPE),
        wout=w_out[l].astype(MXU_DTYPE), ln1_g=row(ln1_g[l]), ln1_b=row(ln1_b[l]),
        wrh=wrh, wrl=(wr - wrh.astype(f32)).astype(MXU_DTYPE),
        br=jnp.concatenate([b_rg[l], b_re[l], jnp.zeros((LANES - N_GROUPS - N_EXPERTS,), f32)]).reshape(1, -1),
        w1=w_e1[l].astype(MXU_DTYPE), w3=w_e3[l].astype(MXU_DTYPE),
        w2=w_e2[l].reshape(-1, d).astype(MXU_DTYPE), ln2_g=row(ln2_g[l]), ln2_b=row(ln2_b[l]),
    )


def kernel(x_prompt, x_sample, cache_a_kv, cache_c_kv, cache_c_kidx, page_table, w_in, w_pa, w_pb, w_pc, w_out, w_s, b_s, ln_v_g, ln_v_b, ln1_g, ln1_b, ln2_g, ln2_b, w_rg, b_rg, w_re, b_re, w_e1, w_e3, w_e2):
    bp, tp, d = x_prompt.shape
    db, ts, _ = x_sample.shape
    depth = w_in.shape[0]
    n_phys, page = cache_a_kv.shape[1], cache_a_kv.shape[2]
    n_pages = page_table.shape[1]
    past = n_pages * page
    assert ts == 1 and tp % MOBA_BLOCK == 0 and cache_a_kv.shape[3:] == (2, N_KV_HEADS, HEAD_DIM)

    cache_a = jnp.transpose(cache_a_kv, (0, 1, 3, 4, 5, 2)).reshape(depth, n_phys, 2 * KW, page)
    cache_c = jnp.transpose(cache_c_kv, (0, 1, 3, 4, 5, 2)).reshape(depth, n_phys, 2 * KW, page)
    cache_i = jnp.transpose(cache_c_kidx, (0, 1, 3, 2))
    page_table = page_table.astype(jnp.int32)

    tm_p = 512 if tp % 512 == 0 else MOBA_BLOCK
    tm_s = db
    pos_p = jnp.arange(tp, dtype=jnp.int32)
    pos_s = jnp.full((db,), past, jnp.int32)
    tabs_p = (_rope_table(pos_p, HEAD_DIM), _rope_table(pos_p, IDX_DIM), _rope_table(pos_p, IDX_DIM, IDX_DIM))
    tabs_s = (_rope_table(pos_s, HEAD_DIM), _rope_table(pos_s, IDX_DIM), _rope_table(pos_s, IDX_DIM, IDX_DIM))

    xp = x_prompt.reshape(bp * tp, d)
    xs = x_sample.reshape(db, d)
    outs = [[] for _ in range(7)]
    eye = jnp.eye(N_KV_HEADS, dtype=f32)
    for l in range(depth):
        p = _layer_params(l, w_in, w_pa, w_pb, w_pc, w_out, w_s, b_s, ln_v_g, ln_v_b, ln1_g, ln1_b, ln2_g, ln2_b,
                          w_rg, b_rg, w_re, b_re, w_e1, w_e3, w_e2)

        qa, qc, kva, kvc, iq, ikw, ob, _, kab, kcb, vta, vtc, ikr = _proj_call(
            xp, p["wm"], p["wg"], *tabs_p, p["lnv_g"], p["lnv_b"], p["ws_chunk"], p["bs_chunk"],
            tm=tm_p, chunked=True)
        oa = _moba_call(qa, kab, vta, batch=bp, t=tp)
        oc = _dsa_call(iq, ikw, ikr, qc, kcb, vtc, batch=bp, t=tp)
        x1 = _merge_call(xp, oa, ob, oc, p["wgate"], p["wpa"], p["wpb"], p["wpc"], p["wout"],
                         p["ln1_g"], p["ln1_b"], tm=tm_p)
        xp = _moe_call(x1, p["wrh"], p["wrl"], p["br"], p["w1"], p["w3"], p["w2"], p["ln2_g"], p["ln2_b"], tm=tm_p)
        outs[0].append(kva.reshape(bp, tp // page, page, 2, N_KV_HEADS, HEAD_DIM))
        outs[1].append(kvc.reshape(bp, tp // page, page, 2, N_KV_HEADS, HEAD_DIM))
        outs[2].append(ikw[:, 0:IDX_DIM].reshape(bp, tp // page, page, IDX_DIM))

        qa, qc, kva, kvc, iq, ikw, ob, vv = _proj_call(
            xs, p["wm"], p["wg"], *tabs_s, p["lnv_g"], p["lnv_b"], p["ws_row"], p["bs_row"],
            tm=tm_s, chunked=False)
        place = lambda q: jnp.einsum("bkgd,kj->bkgjd", q.astype(f32).reshape(db, N_KV_HEADS, GROUP, HEAD_DIM),
                                     eye).reshape(db, N_HEADS, KW).astype(MXU_DTYPE)
        ik_new = ikw[:, 0:IDX_DIM]
        o8a, o8c = _decode_call(
            page_table, cache_a, cache_c, cache_i, l, place(qa), place(qc),
            iq.reshape(db, IDX_HEADS, IDX_DIM), ikw[:, IDX_DIM:IDX_DIM + IDX_HEADS].reshape(db, IDX_HEADS, 1),
            kva.reshape(db, 1, 2 * KW), kvc.reshape(db, 1, 2 * KW), ik_new.reshape(db, 1, IDX_DIM))
        unplace = lambda o: jnp.einsum("bkgkd->bkgd", o.reshape(db, N_KV_HEADS, GROUP, N_KV_HEADS, HEAD_DIM)
                                       ).reshape(db, QW).astype(MXU_DTYPE)
        x1 = _merge_call(xs, unplace(o8a), ob, unplace(o8c), p["wgate"], p["wpa"], p["wpb"], p["wpc"], p["wout"],
                         p["ln1_g"], p["ln1_b"], tm=tm_s)
        xs = _moe_call(x1, p["wrh"], p["wrl"], p["br"], p["w1"], p["w3"], p["w2"], p["ln2_g"], p["ln2_b"], tm=tm_s)
        outs[3].append(kva.reshape(db, 1, 2, N_KV_HEADS, HEAD_DIM))
        outs[4].append(kvc.reshape(db, 1, 2, N_KV_HEADS, HEAD_DIM))
        outs[5].append(ik_new.reshape(db, 1, IDX_DIM))
        outs[6].append(vv.reshape(db, 1, GMLP_WIDTH))

    return (xp.reshape(bp, tp, d), xs.reshape(db, 1, d)) + tuple(jnp.stack(o) for o in outs)
```

```python
import functools
import math

import jax
import jax.numpy as jnp
from jax import lax
from jax.experimental import pallas as pl
from jax.experimental.pallas import tpu as pltpu

HEAD_DIM = 64
N_HEADS = 8
N_KV_HEADS = 4
GROUP = N_HEADS // N_KV_HEADS
MOBA_BLOCK = 256
MOBA_TOPK = 3
IDX_HEADS = 8
IDX_DIM = 32
DSA_TOPK = 256
GMLP_CHUNK = 128
GMLP_GROUPS = 8
GMLP_WIDTH = 512
BRANCH_WIDTH = 512
N_GROUPS = 4
EXPERTS_PER_GROUP = 4
N_EXPERTS = N_GROUPS * EXPERTS_PER_GROUP
ROPE_THETA = 500000.0
ROPE_FRACTION = 4
LN_EPS = 1e-5
NEG_INF = -1e30
DEEPNORM_ALPHA = (2 * 4) ** 0.25

LANES = 128
KEY_TILE = 256
MXU_DTYPE = jnp.bfloat16
VMEM_LIMIT = 56 * 1024 * 1024

QW = N_HEADS * HEAD_DIM
KW = N_KV_HEADS * HEAD_DIM
IQW = IDX_HEADS * IDX_DIM
INT_MIN = -2147483648
HALF_NEG_KEY = -1892283083

f32 = jnp.float32


def _nt_dot(a, b):
    return lax.dot_general(a, b, (((1,), (1,)), ((), ())), preferred_element_type=f32)


def _dot(a, b):
    return jnp.dot(a, b, preferred_element_type=f32)


def _layer_norm(y, g, b):
    mu = jnp.mean(y, axis=-1, keepdims=True)
    d = y - mu
    var = jnp.mean(d * d, axis=-1, keepdims=True)
    return d * lax.rsqrt(var + LN_EPS) * g + b


def _gelu_tanh(x):
    cdf = 0.5 * (1.0 + jnp.tanh(math.sqrt(2.0 / math.pi) * (x + 0.044715 * (x * x * x))))
    return x * cdf


def _sigmoid(x):
    return 1.0 / (1.0 + jnp.exp(-x))


def _sortable(x):
    bits = pltpu.bitcast(x, jnp.int32)
    return jnp.where(bits < 0, bits ^ jnp.int32(0x7FFFFFFF), bits)


def _const_spec(shape):
    nd = len(shape)
    return pl.BlockSpec(shape, lambda *_: (0,) * nd, pipeline_mode=pl.Buffered(1))


def _rope_chunk(h, c, sa, sb, half):
    return h * c + pltpu.roll(h, half, 1) * sa + pltpu.roll(h, LANES - half, 1) * sb


def _proj_body(x_ref, wm_ref, wg_ref, t64_ref, t32_ref, tl_ref, lng_ref, lnb_ref, ws_ref, bs_ref, *out_refs, chunked):
    qa_ref, qc_ref, kva_ref, kvc_ref, iq_ref, ikw_ref, ob_ref, vv_ref = out_refs[:8]
    if chunked:
        kab_ref, kcb_ref, vta_ref, vtc_ref, ikr_ref = out_refs[8:]
    xb = x_ref[...].astype(MXU_DTYPE)
    tm = xb.shape[0]
    c64, sa64, sb64 = t64_ref[0], t64_ref[1], t64_ref[2]
    c32, sa32, sb32 = t32_ref[0], t32_ref[1], t32_ref[2]
    cl, sal, sbl = tl_ref[0], tl_ref[1], tl_ref[2]
    h64 = (HEAD_DIM // ROPE_FRACTION) // 2
    h32 = (IDX_DIM // ROPE_FRACTION) // 2
    scale = HEAD_DIM ** -0.5 * math.log2(math.e)

    hq = _dot(xb, wm_ref[:, 0:2 * QW])
    for c in range(2 * QW // LANES):
        r = _rope_chunk(hq[:, c * LANES:(c + 1) * LANES], c64, sa64, sb64, h64) * scale
        dst = qa_ref if c < QW // LANES else qc_ref
        cc = c % (QW // LANES)
        dst[:, cc * LANES:(cc + 1) * LANES] = r.astype(dst.dtype)

    o = 2 * QW
    hk = _dot(xb, wm_ref[:, o:o + 2 * KW])
    hv = _dot(xb, wm_ref[:, o + 2 * KW:o + 4 * KW])
    for c in range(2 * KW // LANES):
        r = _rope_chunk(hk[:, c * LANES:(c + 1) * LANES], c64, sa64, sb64, h64)
        v = hv[:, c * LANES:(c + 1) * LANES]
        first = c < KW // LANES
        dst = kva_ref if first else kvc_ref
        cc = c % (KW // LANES)
        dst[:, cc * LANES:(cc + 1) * LANES] = r
        dst[:, KW + cc * LANES:KW + (cc + 1) * LANES] = v
        if chunked:
            kb_ref, vt_ref = (kab_ref, vta_ref) if first else (kcb_ref, vtc_ref)
            kb_ref[:, cc * LANES:(cc + 1) * LANES] = r.astype(kb_ref.dtype)
            for j in range(tm // KEY_TILE):
                vt_ref[j, cc * LANES:(cc + 1) * LANES, :] = v[j * KEY_TILE:(j + 1) * KEY_TILE, :].T.astype(vt_ref.dtype)

    o = 2 * QW + 4 * KW
    hi = _dot(xb, wm_ref[:, o:o + IQW + LANES])
    for c in range(IQW // LANES):
        r = _rope_chunk(hi[:, c * LANES:(c + 1) * LANES], c32, sa32, sb32, h32)
        iq_ref[:, c * LANES:(c + 1) * LANES] = r.astype(iq_ref.dtype)
    last = _rope_chunk(hi[:, IQW:IQW + LANES], cl, sal, sbl, h32)
    ikw_ref[...] = last
    if chunked:
        lane = lax.broadcasted_iota(jnp.int32, (tm, LANES), 1)
        t = jnp.where(lane < IDX_DIM, last, 0.0)
        t = t + pltpu.roll(t, IDX_DIM, 1)
        t = t + pltpu.roll(t, 2 * IDX_DIM, 1)
        for c in range(IQW // LANES):
            ikr_ref[:, c * LANES:(c + 1) * LANES] = t.astype(ikr_ref.dtype)

    hg = _dot(xb, wg_ref[...])
    u = _gelu_tanh(hg[:, 0:GMLP_WIDTH])
    vv = _layer_norm(_gelu_tanh(hg[:, GMLP_WIDTH:2 * GMLP_WIDTH]), lng_ref[...], lnb_ref[...])
    vv_ref[...] = vv
    if chunked:
        vb = vv.astype(MXU_DTYPE)
        gd = GMLP_WIDTH // GMLP_GROUPS
        lane_c = lax.broadcasted_iota(jnp.int32, (GMLP_CHUNK, LANES), 1)
        for c in range(tm // GMLP_CHUNK):
            rows = slice(c * GMLP_CHUNK, (c + 1) * GMLP_CHUNK)
            for pr in range(GMLP_WIDTH // LANES):
                vp = vb[rows, pr * LANES:(pr + 1) * LANES]
                g0 = (pr * LANES) // gd
                r0 = _dot(ws_ref[g0], vp)
                r1 = _dot(ws_ref[g0 + 1], vp)
                s = jnp.where(lane_c < gd, r0, r1) + bs_ref[:, pr * LANES:(pr + 1) * LANES]
                ob_ref[rows, pr * LANES:(pr + 1) * LANES] = (u[rows, pr * LANES:(pr + 1) * LANES] * s).astype(ob_ref.dtype)
    else:
        ob_ref[...] = (u * (vv * ws_ref[...] + bs_ref[...])).astype(ob_ref.dtype)


def _proj_call(x, wm, wg, t64, t32, tl, lng, lnb, ws, bs, *, tm, chunked):
    n, d = x.shape
    nt = t64.shape[1] // tm
    grid = (n // tm,)
    row = lambda w: pl.BlockSpec((tm, w), lambda i: (i, 0))
    tab = pl.BlockSpec((3, tm, LANES), lambda i: (0, i % nt, 0))
    outs = [
        (QW, MXU_DTYPE), (QW, MXU_DTYPE),
        (2 * KW, f32), (2 * KW, f32),
        (IQW, MXU_DTYPE),
        (LANES, f32),
        (GMLP_WIDTH, MXU_DTYPE),
        (GMLP_WIDTH, f32),
    ]
    out_specs = [row(w) for w, _ in outs]
    out_shape = [jax.ShapeDtypeStruct((n, w), dt) for w, dt in outs]
    if chunked:
        assert tm % KEY_TILE == 0
        vt_spec = pl.BlockSpec((tm // KEY_TILE, KW, KEY_TILE), lambda i: (i, 0, 0))
        vt_shape = jax.ShapeDtypeStruct((n // KEY_TILE, KW, KEY_TILE), MXU_DTYPE)
        out_specs += [row(KW), row(KW), vt_spec, vt_spec, row(IQW)]
        out_shape += [jax.ShapeDtypeStruct((n, KW), MXU_DTYPE)] * 2 + [vt_shape] * 2 \
            + [jax.ShapeDtypeStruct((n, IQW), MXU_DTYPE)]
    return pl.pallas_call(
        functools.partial(_proj_body, chunked=chunked),
        grid=grid,
        in_specs=[row(d), _const_spec(wm.shape), _const_spec(wg.shape), tab, tab, tab,
                  _const_spec(lng.shape), _const_spec(lnb.shape), _const_spec(ws.shape), _const_spec(bs.shape)],
        out_specs=out_specs,
        out_shape=out_shape,
        compiler_params=pltpu.CompilerParams(dimension_semantics=("parallel",), vmem_limit_bytes=VMEM_LIMIT),
        name="proj",
    )(x, wm, wg, t64, t32, tl, lng, lnb, ws, bs)


def _head_rows(q_ref, kvh, tq):
    qp = q_ref[:, kvh * LANES:(kvh + 1) * LANES].astype(f32)
    rl = pltpu.roll(qp, HEAD_DIM, 1)
    lo = lax.broadcasted_iota(jnp.int32, (tq, LANES), 1) < HEAD_DIM
    if kvh % 2 == 0:
        g0, g1 = jnp.where(lo, qp, 0.0), jnp.where(lo, rl, 0.0)
    else:
        g0, g1 = jnp.where(lo, 0.0, rl), jnp.where(lo, 0.0, qp)
    return jnp.concatenate([g0, g1], axis=0).astype(MXU_DTYPE)


ACC_ROWS = HEAD_DIM + 16


def _value_rows(vt_ref, kt, kvh):
    vt = vt_ref[kt, kvh * HEAD_DIM:(kvh + 1) * HEAD_DIM, :]
    return jnp.concatenate([vt, jnp.ones((ACC_ROWS - HEAD_DIM, vt.shape[1]), vt.dtype)], axis=0)


def _attend_tile(k_ref, vt_ref, qh, acc_sc, s_sc, state, kt, masks):
    start = pl.multiple_of(kt * KEY_TILE, KEY_TILE)
    for kvh in range(N_KV_HEADS):
        pc = kvh // 2
        s_sc[kvh] = _nt_dot(k_ref[pl.ds(start, KEY_TILE), pc * LANES:(pc + 1) * LANES], qh[kvh])
    out = []
    for kvh in range(N_KV_HEADS):
        m, acc = _softmax_step(s_sc[kvh], masks[kvh], state[kvh], acc_sc[kvh], _value_rows(vt_ref, kt, kvh))
        acc_sc[kvh] = acc
        out.append(m)
    return tuple(out)


def _head_out(acc, tq):
    o = acc[0:HEAD_DIM] / acc[HEAD_DIM:HEAD_DIM + 1]
    return jnp.concatenate([o[:, g * tq:(g + 1) * tq] for g in range(GROUP)], axis=0).T


def _softmax_step(s, mask, m, acc, vt):
    s = jnp.where(mask, s, NEG_INF)
    m_new = jnp.maximum(m, jnp.max(s, axis=0, keepdims=True))
    p = jnp.exp2(s - m_new)
    acc_new = jnp.exp2(m - m_new) * acc + _dot(vt, p.astype(MXU_DTYPE))
    return m_new, acc_new


def _moba_body(q_ref, k_ref, vt_ref, o_ref, kmean_sc, sel_sc, acc_sc, s_sc, *, tq):
    i = pl.program_id(1)
    nb = k_ref.shape[0] // MOBA_BLOCK
    nbp = kmean_sc.shape[0]
    cols = GROUP * tq

    @pl.when(i == 0)
    def _():
        kmean_sc[...] = jnp.zeros_like(kmean_sc)
        for n in range(nb):
            blk = k_ref[n * MOBA_BLOCK:(n + 1) * MOBA_BLOCK, :].astype(f32)
            kmean_sc[n:n + 1, :] = jnp.mean(blk, axis=0, keepdims=True)

    blk_id = lax.broadcasted_iota(jnp.int32, (nbp, cols), 0)
    qh = [_head_rows(q_ref, kvh, tq) for kvh in range(N_KV_HEADS)]

    for kvh in range(N_KV_HEADS):
        pc = kvh // 2
        gate = _nt_dot(kmean_sc[:, pc * LANES:(pc + 1) * LANES].astype(MXU_DTYPE), qh[kvh])
        g = jnp.where(blk_id < i, gate, NEG_INF)
        sel = jnp.zeros((nbp, cols), f32)
        for _ in range(MOBA_TOPK):
            mx = jnp.max(g, axis=0, keepdims=True)
            first = jnp.min(jnp.where(g == mx, blk_id, 1 << 20), axis=0, keepdims=True)
            hit = blk_id == first
            sel = jnp.where(hit & (mx > 0.5 * NEG_INF), 1.0, sel)
            g = jnp.where(hit, -3e38, g)
        sel_sc[kvh] = sel

    acc_sc[...] = jnp.zeros_like(acc_sc)
    key_in_blk = lax.broadcasted_iota(jnp.int32, (KEY_TILE, cols), 0)
    q_in_blk = lax.broadcasted_iota(jnp.int32, (KEY_TILE, cols), 1) % tq
    state = tuple(jnp.full((1, cols), NEG_INF, f32) for _ in range(N_KV_HEADS))
    state = _attend_tile(k_ref, vt_ref, qh, acc_sc, s_sc, state, i, [key_in_blk <= q_in_blk] * N_KV_HEADS)

    def past(n, state):
        chosen = [jnp.broadcast_to(sel_sc[kvh, pl.ds(n, 1), :] > 0.5, (KEY_TILE, cols)) for kvh in range(N_KV_HEADS)]
        return _attend_tile(k_ref, vt_ref, qh, acc_sc, s_sc, state, n, chosen)

    lax.fori_loop(0, i, past, state)
    for kvh in range(N_KV_HEADS):
        o_ref[:, kvh * LANES:(kvh + 1) * LANES] = _head_out(acc_sc[kvh], tq).astype(o_ref.dtype)


def _moba_call(q, kb, vt, *, batch, t):
    tq = MOBA_BLOCK
    assert tq == KEY_TILE and GROUP * HEAD_DIM == LANES
    nq = t // tq
    nbp = -(-nq // 16) * 16
    return pl.pallas_call(
        functools.partial(_moba_body, tq=tq),
        grid=(batch, nq),
        in_specs=[pl.BlockSpec((tq, QW), lambda b, i: (b * nq + i, 0)),
                  pl.BlockSpec((t, KW), lambda b, i: (b, 0)),
                  pl.BlockSpec((nq, KW, KEY_TILE), lambda b, i: (b, 0, 0))],
        out_specs=pl.BlockSpec((tq, QW), lambda b, i: (b * nq + i, 0)),
        out_shape=jax.ShapeDtypeStruct(q.shape, MXU_DTYPE),
        scratch_shapes=[pltpu.VMEM((nbp, KW), f32), pltpu.VMEM((N_KV_HEADS, nbp, GROUP * tq), f32),
                        pltpu.VMEM((N_KV_HEADS, ACC_ROWS, GROUP * tq), f32),
                        pltpu.VMEM((N_KV_HEADS, KEY_TILE, GROUP * tq), f32)],
        compiler_params=pltpu.CompilerParams(dimension_semantics=("parallel", "arbitrary"),
                                             vmem_limit_bytes=VMEM_LIMIT),
        name="moba",
    )(q, kb, vt)


def _select_threshold(count_ge, n_top, rows):
    def step(it, t_u):
        cand_u = t_u | lax.shift_left(jnp.int32(1), jnp.int32(31) - it)
        cnt = count_ge(cand_u ^ jnp.int32(INT_MIN))
        return jnp.where(cnt >= n_top, cand_u, t_u)
    t_u = lax.fori_loop(0, 32, step, jnp.zeros((1, rows), jnp.int32))
    return t_u ^ jnp.int32(INT_MIN)


def _dsa_body(iq_ref, ikw_ref, ikr_ref, q_ref, k_ref, vt_ref, o_ref, key_sc, cut_sc, acc_sc, s_sc,
              *, tq, n_top, idx_bits):
    i = pl.program_id(1)
    ntile = i + 1
    cols = GROUP * tq
    lane_q = lax.broadcasted_iota(jnp.int32, (tq, IQW), 1)
    krow = lax.broadcasted_iota(jnp.int32, (tq, tq), 0)
    qcol = lax.broadcasted_iota(jnp.int32, (tq, tq), 1)
    half_neg = jnp.int32(HALF_NEG_KEY)

    iqv = iq_ref[...]
    iqm = [jnp.where((lane_q >= h * IDX_DIM) & (lane_q < (h + 1) * IDX_DIM), iqv, jnp.zeros_like(iqv))
           for h in range(IDX_HEADS)]
    iwt = ikw_ref[...].T
    iw = [iwt[IDX_DIM + h:IDX_DIM + h + 1, :] for h in range(IDX_HEADS)]
    sc_scale = (IDX_DIM ** -0.5) * (IDX_HEADS ** -0.5)

    def score_tile(kt, _):
        ik_t = ikr_ref[pl.ds(pl.multiple_of(kt * tq, tq), tq), :]
        sc = jnp.zeros((tq, tq), f32)
        for h in range(IDX_HEADS):
            sc = sc + iw[h] * jnp.maximum(_nt_dot(ik_t, iqm[h]), 0.0)
        sc = sc * sc_scale
        sc = jnp.where(kt * tq + krow <= i * tq + qcol, sc, NEG_INF)
        key_sc[kt] = _sortable(sc)
        return 0

    lax.fori_loop(0, ntile, score_tile, 0)

    def count_where(pred):
        def body(kt, c8):
            hit = jnp.where(pred(key_sc[kt], kt), 1, 0)
            return c8 + jnp.sum(hit.reshape(tq // 8, 8, tq), axis=0)
        c8 = lax.fori_loop(0, ntile, body, jnp.zeros((8, tq), jnp.int32))
        return jnp.sum(c8, axis=0, keepdims=True)

    thr = _select_threshold(lambda cand: count_where(lambda k, kt: k >= cand), n_top, tq)
    cnt_gt = count_where(lambda k, kt: k > thr)
    cnt_eq = count_where(lambda k, kt: k == thr)
    need = n_top - cnt_gt

    cut_sc[...] = jnp.full(cut_sc.shape, 1 << 30, jnp.int32)
    excess = (cnt_eq > need) & (thr > half_neg)

    @pl.when(jnp.max(jnp.where(excess, 1, 0)) > 0)
    def _():
        def step(it, t_i):
            cand = t_i | lax.shift_left(jnp.int32(1), jnp.int32(idx_bits - 1) - it)
            cnt = count_where(lambda k, kt: (k == thr) & (kt * tq + krow < cand))
            return jnp.where(cnt < need, cand, t_i)
        t_i = lax.fori_loop(0, idx_bits, step, jnp.zeros((1, tq), jnp.int32))
        cut_sc[...] = jnp.broadcast_to(jnp.where(excess, t_i, 1 << 30), cut_sc.shape)

    cut = cut_sc[0:1, :]

    qh = [_head_rows(q_ref, kvh, tq) for kvh in range(N_KV_HEADS)]
    acc_sc[...] = jnp.zeros_like(acc_sc)

    def attend(kt, state):
        k = key_sc[kt]
        keep = ((k > thr) | ((k == thr) & (kt * tq + krow <= cut))) & (k > half_neg)
        mask = jnp.concatenate([keep] * GROUP, axis=1)
        return _attend_tile(k_ref, vt_ref, qh, acc_sc, s_sc, state, kt, [mask] * N_KV_HEADS)

    lax.fori_loop(0, ntile, attend, tuple(jnp.full((1, cols), NEG_INF, f32) for _ in range(N_KV_HEADS)))
    for kvh in range(N_KV_HEADS):
        o_ref[:, kvh * LANES:(kvh + 1) * LANES] = _head_out(acc_sc[kvh], tq).astype(o_ref.dtype)


def _dsa_call(iq, ikw, ikr, q, kb, vt, *, batch, t):
    tq = KEY_TILE
    nq = t // tq
    n_top = min(DSA_TOPK, t // 4)
    assert n_top <= tq and t % tq == 0 and GROUP * HEAD_DIM == LANES
    rowspec = lambda w: pl.BlockSpec((tq, w), lambda b, i: (b * nq + i, 0))
    seqspec = lambda w: pl.BlockSpec((t, w), lambda b, i: (b, 0))
    return pl.pallas_call(
        functools.partial(_dsa_body, tq=tq, n_top=n_top, idx_bits=max(1, (t - 1).bit_length())),
        grid=(batch, nq),
        in_specs=[rowspec(IQW), rowspec(LANES), seqspec(IQW), rowspec(QW), seqspec(KW),
                  pl.BlockSpec((nq, KW, KEY_TILE), lambda b, i: (b, 0, 0))],
        out_specs=rowspec(QW),
        out_shape=jax.ShapeDtypeStruct(q.shape, MXU_DTYPE),
        scratch_shapes=[pltpu.VMEM((nq, tq, tq), jnp.int32), pltpu.VMEM((8, tq), jnp.int32),
                        pltpu.VMEM((N_KV_HEADS, ACC_ROWS, GROUP * tq), f32),
                        pltpu.VMEM((N_KV_HEADS, KEY_TILE, GROUP * tq), f32)],
        compiler_params=pltpu.CompilerParams(dimension_semantics=("parallel", "arbitrary"),
                                             vmem_limit_bytes=VMEM_LIMIT),
        name="dsa",
    )(iq, ikw, ikr, q, kb, vt)


RADIX_BITS = 4


def _radix_select(count_fn, total_bits, top_bit):
    digit = lax.broadcasted_iota(jnp.int32, (1 << RADIX_BITS, 1), 0)
    t = jnp.zeros((1, 1), jnp.int32)
    for step in range(total_bits // RADIX_BITS):
        shift = top_bit + 1 - RADIX_BITS * (step + 1)
        ok = count_fn(t | lax.shift_left(digit, jnp.int32(shift)))
        best = jnp.sum(jnp.where(ok, 1, 0), axis=0, keepdims=True) - 1
        t = t | lax.shift_left(best, jnp.int32(shift))
    return t


def _decode_body(pt_ref, ca_hbm, cc_hbm, ci_hbm, qa_ref, qc_ref, iq_ref, iw_ref, na_ref, nc_ref, ni_ref,
                 oa_ref, oc_ref, abuf, cbuf, ibuf, sems, *, layer, n_pages, page, n_top):
    b = pl.program_id(0)
    slot = lax.rem(b, 2)
    past = n_pages * page
    pages_per_block = MOBA_BLOCK // page
    nbp = n_pages // pages_per_block

    def page_copies(seq, sl):
        out = []
        for p in range(n_pages):
            pg = pt_ref[seq, p]
            out.append(pltpu.make_async_copy(ca_hbm.at[layer, pg], abuf.at[sl, p], sems.at[0, sl]))
            out.append(pltpu.make_async_copy(cc_hbm.at[layer, pg], cbuf.at[sl, p], sems.at[1, sl]))
            out.append(pltpu.make_async_copy(ci_hbm.at[layer, pg], ibuf.at[sl, p], sems.at[2, sl]))
        return out

    @pl.when(b == 0)
    def _():
        for c in page_copies(0, 0):
            c.start()

    @pl.when(b + 1 < pl.num_programs(0))
    def _():
        for c in page_copies(b + 1, 1 - slot):
            c.start()

    for c in page_copies(b, slot):
        c.wait()

    lane = lax.broadcasted_iota(jnp.int32, (N_HEADS, LANES), 1)

    def finish(s_list, mask_list, q8, knew, vnew, vbuf, extra_ok):
        s_new = jnp.sum(q8.astype(f32) * knew, axis=1, keepdims=True)
        s_new = jnp.where(extra_ok, s_new, NEG_INF)
        sm = [jnp.where(mk, s, NEG_INF) for s, mk in zip(s_list, mask_list)]
        m = s_new
        for s in sm:
            m = jnp.maximum(m, jnp.max(s, axis=1, keepdims=True))
        p_new = jnp.where(extra_ok, jnp.exp2(s_new - m), 0.0)
        l = p_new
        acc = p_new * vnew
        for p, (s, mk) in enumerate(zip(sm, mask_list)):
            pc = jnp.where(mk, jnp.exp2(s - m), 0.0)
            l = l + jnp.sum(pc, axis=1, keepdims=True)
            acc = acc + _nt_dot(pc.astype(MXU_DTYPE), vbuf[slot, p, KW:2 * KW, :].astype(MXU_DTYPE))
        return acc / l

    q8 = qa_ref[...]
    sa = [_dot(q8, abuf[slot, p, 0:KW, :].astype(MXU_DTYPE)) for p in range(n_pages)]
    gate = jnp.full((N_HEADS, LANES), NEG_INF, f32)
    for n in range(nbp):
        tot = sum(jnp.sum(sa[n * pages_per_block + j], axis=1, keepdims=True) for j in range(pages_per_block))
        gate = jnp.where(lane == n, tot * (1.0 / MOBA_BLOCK), gate)
    sel = jnp.zeros((N_HEADS, LANES), f32)
    g = gate
    for _ in range(min(MOBA_TOPK, nbp + 1)):
        mx = jnp.max(g, axis=1, keepdims=True)
        first = jnp.min(jnp.where(g == mx, lane, 1 << 20), axis=1, keepdims=True)
        hit = lane == first
        sel = jnp.where(hit & (mx > 0.5 * NEG_INF), 1.0, sel)
        g = jnp.where(hit, -3e38, g)
    chosen = [jnp.max(jnp.where(lane == n, sel, 0.0), axis=1, keepdims=True) > 0.5 for n in range(nbp)]
    masks = [jnp.broadcast_to(chosen[p // pages_per_block], sa[p].shape) for p in range(n_pages)]
    oa_ref[...] = finish(sa, masks, q8, na_ref[:, 0:KW], na_ref[:, KW:2 * KW], abuf, True)

    sc_scale = (IDX_DIM ** -0.5) * (IDX_HEADS ** -0.5)
    iq8 = iq_ref[...]
    iw8 = iw_ref[...]
    sc = jnp.concatenate(
        [jnp.sum(iw8 * jnp.maximum(_dot(iq8, ibuf[slot, p].astype(MXU_DTYPE)), 0.0), axis=0, keepdims=True)
         for p in range(n_pages)], axis=1) * sc_scale
    rel_n = jnp.maximum(jnp.sum(iq8.astype(f32) * ni_ref[...], axis=1, keepdims=True), 0.0)
    sc_n = jnp.sum(iw8 * rel_n, axis=0, keepdims=True) * sc_scale
    key = _sortable(sc)
    key_n = _sortable(sc_n)
    idx = lax.broadcasted_iota(jnp.int32, (1, past), 1)
    nd = 1 << RADIX_BITS
    key_b = jnp.broadcast_to(key, (nd, past))
    idx_b = jnp.broadcast_to(idx, (nd, past))

    def count(pred_past, pred_new):
        return jnp.sum(jnp.where(pred_past, 1, 0), axis=1, keepdims=True) + jnp.where(pred_new, 1, 0)

    def enough(cand_u):
        cand = cand_u ^ jnp.int32(INT_MIN)
        return count(key_b >= cand, key_n >= cand) >= n_top

    thr = _radix_select(enough, 32, 31) ^ jnp.int32(INT_MIN)
    need = n_top - count(key > thr, key_n > thr)
    eq = key == thr
    eq_b = jnp.broadcast_to(eq, (nd, past))
    idx_bits = -(-max(1, past.bit_length()) // RADIX_BITS) * RADIX_BITS
    cut = _radix_select(lambda cand: jnp.sum(jnp.where(eq_b & (idx_b < cand), 1, 0), axis=1, keepdims=True) < need,
                        idx_bits, idx_bits - 1)
    eq_past = jnp.sum(jnp.where(eq, 1, 0), axis=1, keepdims=True)
    keep = (key > thr) | (eq & (idx <= cut))
    keep_n = (key_n > thr) | ((key_n == thr) & (eq_past < need))
    q8c = qc_ref[...]
    s_c = [_dot(q8c, cbuf[slot, p, 0:KW, :].astype(MXU_DTYPE)) for p in range(n_pages)]
    masks_c = [jnp.broadcast_to(keep[:, p * page:(p + 1) * page], s_c[p].shape) for p in range(n_pages)]
    oc_ref[...] = finish(s_c, masks_c, q8c, nc_ref[:, 0:KW], nc_ref[:, KW:2 * KW], cbuf, keep_n)


def _decode_call(page_table, cache_at, cache_ct, cache_it, layer, q8a, q8c, iq8, iw8, kva, kvc, ik):
    db, n_pages = page_table.shape
    page = cache_at.shape[3]
    past = n_pages * page
    assert past % MOBA_BLOCK == 0 and MOBA_BLOCK % page == 0 and page == LANES
    n_top = min(DSA_TOPK, (past + 1) // 4)
    assert n_top <= past
    seqspec = lambda r, w: pl.BlockSpec((None, r, w), lambda b, pt: (b, 0, 0))
    hbm = pl.BlockSpec(memory_space=pl.ANY)
    grid_spec = pltpu.PrefetchScalarGridSpec(
        num_scalar_prefetch=1,
        grid=(db,),
        in_specs=[hbm, hbm, hbm,
                  seqspec(N_HEADS, KW), seqspec(N_HEADS, KW), seqspec(IDX_HEADS, IDX_DIM), seqspec(IDX_HEADS, 1),
                  seqspec(1, 2 * KW), seqspec(1, 2 * KW), seqspec(1, IDX_DIM)],
        out_specs=[seqspec(N_HEADS, KW), seqspec(N_HEADS, KW)],
        scratch_shapes=[pltpu.VMEM((2, n_pages, 2 * KW, page), f32), pltpu.VMEM((2, n_pages, 2 * KW, page), f32),
                        pltpu.VMEM((2, n_pages, IDX_DIM, page), f32), pltpu.SemaphoreType.DMA((3, 2))],
    )
    return pl.pallas_call(
        functools.partial(_decode_body, layer=layer, n_pages=n_pages, page=page, n_top=n_top),
        grid_spec=grid_spec,
        out_shape=[jax.ShapeDtypeStruct((db, N_HEADS, KW), f32)] * 2,
        compiler_params=pltpu.CompilerParams(dimension_semantics=("arbitrary",), vmem_limit_bytes=VMEM_LIMIT),
        name="decode",
    )(page_table, cache_at, cache_ct, cache_it, q8a, q8c, iq8, iw8, kva, kvc, ik)


def _merge_body(x_ref, oa_ref, ob_ref, oc_ref, wgate_ref, wpa_ref, wpb_ref, wpc_ref, wout_ref, g_ref, b_ref, o_ref):
    x = x_ref[...]
    d = x.shape[1]
    xb = x.astype(MXU_DTYPE)
    merged = None
    for k, (o_r, wp_r) in enumerate(((oa_ref, wpa_ref), (ob_ref, wpb_ref), (oc_ref, wpc_ref))):
        gate = _sigmoid(_dot(xb, wgate_ref[:, k * d:(k + 1) * d]))
        term = gate * _dot(o_r[...].astype(MXU_DTYPE), wp_r[...])
        merged = term if merged is None else merged + term
    y = DEEPNORM_ALPHA * x + _dot(merged.astype(MXU_DTYPE), wout_ref[...])
    o_ref[...] = _layer_norm(y, g_ref[...], b_ref[...])


def _merge_call(x, oa, ob, oc, wgate, wpa, wpb, wpc, wout, g, b, *, tm):
    n, d = x.shape
    row = lambda w: pl.BlockSpec((tm, w), lambda i: (i, 0))
    return pl.pallas_call(
        _merge_body,
        grid=(n // tm,),
        in_specs=[row(d), row(BRANCH_WIDTH), row(BRANCH_WIDTH), row(BRANCH_WIDTH)]
        + [_const_spec(a.shape) for a in (wgate, wpa, wpb, wpc, wout, g, b)],
        out_specs=row(d),
        out_shape=jax.ShapeDtypeStruct((n, d), f32),
        compiler_params=pltpu.CompilerParams(dimension_semantics=("parallel",), vmem_limit_bytes=VMEM_LIMIT),
        name="merge",
    )(x, oa, ob, oc, wgate, wpa, wpb, wpc, wout, g, b)


def _moe_body(x_ref, wrh_ref, wrl_ref, br_ref, w1_ref, w3_ref, w2_ref, g_ref, b_ref, o_ref, hh_sc):
    x = x_ref[...]
    tm = x.shape[0]
    xb = x.astype(MXU_DTYPE)
    xl = (x - xb.astype(f32)).astype(MXU_DTYPE)
    logits = _dot(xb, wrh_ref[...]) + _dot(xb, wrl_ref[...]) + _dot(xl, wrh_ref[...]) + br_ref[...]
    lane = lax.broadcasted_iota(jnp.int32, (tm, LANES), 1)
    big = 1 << 20
    gl = jnp.where(lane < N_GROUPS, logits, -jnp.inf)
    gmax = jnp.max(gl, axis=1, keepdims=True)
    g_sel = jnp.min(jnp.where(gl == gmax, lane, big), axis=1, keepdims=True)
    g_w = 1.0 / jnp.sum(jnp.exp(gl - gmax), axis=1, keepdims=True)
    lo = N_GROUPS + EXPERTS_PER_GROUP * g_sel
    el = jnp.where((lane >= lo) & (lane < lo + EXPERTS_PER_GROUP), logits, -jnp.inf)
    m1 = jnp.max(el, axis=1, keepdims=True)
    i1 = jnp.min(jnp.where(el == m1, lane, big), axis=1, keepdims=True)
    el2 = jnp.where(lane == i1, -jnp.inf, el)
    m2 = jnp.max(el2, axis=1, keepdims=True)
    i2 = jnp.min(jnp.where(el2 == m2, lane, big), axis=1, keepdims=True)
    e2 = jnp.exp(m2 - m1)
    w_first = g_w / (1.0 + e2)
    w_second = g_w * e2 / (1.0 + e2)
    comb = jnp.where(lane == i1, w_first, 0.0) + jnp.where(lane == i2, w_second, 0.0)

    ff = w1_ref.shape[2]
    for e in range(N_EXPERTS):
        hg = _dot(xb, w1_ref[e])
        hu = _dot(xb, w3_ref[e])
        ce = comb[:, N_GROUPS + e:N_GROUPS + e + 1]
        hh_sc[:, e * ff:(e + 1) * ff] = (hg * _sigmoid(hg) * hu * ce).astype(hh_sc.dtype)
    y = DEEPNORM_ALPHA * x + _dot(hh_sc[...], w2_ref[...])
    o_ref[...] = _layer_norm(y, g_ref[...], b_ref[...])


def _moe_call(x, wrh, wrl, br, w1, w3, w2, g, b, *, tm):
    n, d = x.shape
    row = pl.BlockSpec((tm, d), lambda i: (i, 0))
    return pl.pallas_call(
        _moe_body,
        grid=(n // tm,),
        in_specs=[row] + [_const_spec(a.shape) for a in (wrh, wrl, br, w1, w3, w2, g, b)],
        out_specs=row,
        out_shape=jax.ShapeDtypeStruct((n, d), f32),
        scratch_shapes=[pltpu.VMEM((tm, w2.shape[0]), MXU_DTYPE)],
        compiler_params=pltpu.CompilerParams(dimension_semantics=("parallel",), vmem_limit_bytes=VMEM_LIMIT),
        name="moe",
    )(x, wrh, wrl, br, w1, w3, w2, g, b)


def _rope_table(pos, dim, ident_from=None):
    rot = dim // ROPE_FRACTION
    half = rot // 2
    inv_freq = ROPE_THETA ** (-jnp.arange(half, dtype=f32) / half)
    ang = pos.astype(f32)[:, None] * inv_freq[None, :]
    cos, sin = jnp.cos(ang), jnp.sin(ang)
    n = pos.shape[0]
    z = lambda w: jnp.zeros((n, w), f32)
    c = jnp.concatenate([cos, cos, jnp.ones((n, dim - rot), f32)], axis=1)
    sa = jnp.concatenate([z(half), sin, z(dim - rot)], axis=1)
    sb = jnp.concatenate([-sin, z(dim - half)], axis=1)
    tab = jnp.stack([jnp.tile(a, (1, LANES // dim)) for a in (c, sa, sb)])
    if ident_from is not None:
        keep = (jnp.arange(LANES) < ident_from)[None, None, :]
        ident = jnp.stack([jnp.ones((n, LANES), f32), jnp.zeros((n, LANES), f32), jnp.zeros((n, LANES), f32)])
        tab = jnp.where(keep, tab, ident)
    return tab


def _layer_params(l, w_in, w_pa, w_pb, w_pc, w_out, w_s, b_s, ln_v_g, ln_v_b, ln1_g, ln1_b, ln2_g, ln2_b,
                  w_rg, b_rg, w_re, b_re, w_e1, w_e3, w_e2):
    d = w_in.shape[1]
    w = w_in[l]
    offs = {}
    o = 0
    for name, width in (("aq", QW), ("ak", KW), ("av", KW), ("cq", QW), ("ck", KW), ("cv", KW),
                        ("iq", IQW), ("ik", IDX_DIM), ("iw", IDX_HEADS), ("gu", GMLP_WIDTH), ("gv", GMLP_WIDTH),
                        ("ga", d), ("gb", d), ("gc", d)):
        offs[name] = (o, o + width)
        o += width
    assert o == w.shape[1]
    cols = lambda n: w[:, offs[n][0]:offs[n][1]]
    pad = jnp.zeros((d, LANES - IDX_DIM - IDX_HEADS), w.dtype)
    wm = jnp.concatenate([cols("aq"), cols("cq"), cols("ak"), cols("ck"), cols("av"), cols("cv"),
                          cols("iq"), cols("ik"), cols("iw"), pad], axis=1).astype(MXU_DTYPE)
    wg = jnp.concatenate([cols("gu"), cols("gv")], axis=1).astype(MXU_DTYPE)
    wgate = jnp.concatenate([cols("ga"), cols("gb"), cols("gc")], axis=1).astype(MXU_DTYPE)
    causal = jnp.tril(jnp.ones((GMLP_CHUNK, GMLP_CHUNK), dtype=bool))
    gd = GMLP_WIDTH // GMLP_GROUPS
    ws = jnp.where(causal[None], w_s[l], 0)
    wr = jnp.concatenate([w_rg[l], w_re[l], jnp.zeros((d, LANES - N_GROUPS - N_EXPERTS), f32)], axis=1)
    wrh = wr.astype(MXU_DTYPE)
    row = lambda a: a.reshape(1, -1)
    return dict(
        wm=wm, wg=wg, wgate=wgate,
        ws_chunk=ws.astype(MXU_DTYPE),
        bs_chunk=jnp.repeat(b_s[l].T, gd, axis=1),
        ws_row=jnp.repeat(w_s[l][:, 0, 0], gd).reshape(1, -1),
        bs_row=jnp.repeat(b_s[l][:, 0], gd).reshape(1, -1),
        lnv_g=row(ln_v_g[l]), lnv_b=row(ln_v_b[l]),
        wpa=w_pa[l].astype(MXU_DTYPE), wpb=w_pb[l].astype(MXU_DTYPE), wpc=w_pc[l].astype(MXU_DTYPE),
        wout=w_out[l].astype(MXU_DTYPE), ln1_g=row(ln1_g[l]), ln1_b=row(ln1_b[l]),
        wrh=wrh, wrl=(wr - wrh.astype(f32)).astype(MXU_DTYPE),
        br=jnp.concatenate([b_rg[l], b_re[l], jnp.zeros((LANES - N_GROUPS - N_EXPERTS,), f32)]).reshape(1, -1),
        w1=w_e1[l].astype(MXU_DTYPE), w3=w_e3[l].astype(MXU_DTYPE),
        w2=w_e2[l].reshape(-1, d).astype(MXU_DTYPE), ln2_g=row(ln2_g[l]), ln2_b=row(ln2_b[l]),
    )


def kernel(x_prompt, x_sample, cache_a_kv, cache_c_kv, cache_c_kidx, page_table, w_in, w_pa, w_pb, w_pc, w_out, w_s, b_s, ln_v_g, ln_v_b, ln1_g, ln1_b, ln2_g, ln2_b, w_rg, b_rg, w_re, b_re, w_e1, w_e3, w_e2):
    bp, tp, d = x_prompt.shape
    db, ts, _ = x_sample.shape
    depth = w_in.shape[0]
    n_phys, page = cache_a_kv.shape[1], cache_a_kv.shape[2]
    n_pages = page_table.shape[1]
    past = n_pages * page
    assert ts == 1 and tp % MOBA_BLOCK == 0 and cache_a_kv.shape[3:] == (2, N_KV_HEADS, HEAD_DIM)

    cache_a = jnp.transpose(cache_a_kv, (0, 1, 3, 4, 5, 2)).reshape(depth, n_phys, 2 * KW, page)
    cache_c = jnp.transpose(cache_c_kv, (0, 1, 3, 4, 5, 2)).reshape(depth, n_phys, 2 * KW, page)
    cache_i = jnp.transpose(cache_c_kidx, (0, 1, 3, 2))
    page_table = page_table.astype(jnp.int32)

    tm_p = 512 if tp % 512 == 0 else MOBA_BLOCK
    tm_s = db
    pos_p = jnp.arange(tp, dtype=jnp.int32)
    pos_s = jnp.full((db,), past, jnp.int32)
    tabs_p = (_rope_table(pos_p, HEAD_DIM), _rope_table(pos_p, IDX_DIM), _rope_table(pos_p, IDX_DIM, IDX_DIM))
    tabs_s = (_rope_table(pos_s, HEAD_DIM), _rope_table(pos_s, IDX_DIM), _rope_table(pos_s, IDX_DIM, IDX_DIM))

    xp = x_prompt.reshape(bp * tp, d)
    xs = x_sample.reshape(db, d)
    outs = [[] for _ in range(7)]
    eye = jnp.eye(N_KV_HEADS, dtype=f32)
    for l in range(depth):
        p = _layer_params(l, w_in, w_pa, w_pb, w_pc, w_out, w_s, b_s, ln_v_g, ln_v_b, ln1_g, ln1_b, ln2_g, ln2_b,
                          w_rg, b_rg, w_re, b_re, w_e1, w_e3, w_e2)

        qa, qc, kva, kvc, iq, ikw, ob, _, kab, kcb, vta, vtc, ikr = _proj_call(
            xp, p["wm"], p["wg"], *tabs_p, p["lnv_g"], p["lnv_b"], p["ws_chunk"], p["bs_chunk"],
            tm=tm_p, chunked=True)
        oa = _moba_call(qa, kab, vta, batch=bp, t=tp)
        oc = _dsa_call(iq, ikw, ikr, qc, kcb, vtc, batch=bp, t=tp)
        x1 = _merge_call(xp, oa, ob, oc, p["wgate"], p["wpa"], p["wpb"], p["wpc"], p["wout"],
                         p["ln1_g"], p["ln1_b"], tm=tm_p)
        xp = _moe_call(x1, p["wrh"], p["wrl"], p["br"], p["w1"], p["w3"], p["w2"], p["ln2_g"], p["ln2_b"], tm=tm_p)
        outs[0].append(kva.reshape(bp, tp // page, page, 2, N_KV_HEADS, HEAD_DIM))
        outs[1].append(kvc.reshape(bp, tp // page, page, 2, N_KV_HEADS, HEAD_DIM))
        outs[2].append(ikw[:, 0:IDX_DIM].reshape(bp, tp // page, page, IDX_DIM))

        qa, qc, kva, kvc, iq, ikw, ob, vv = _proj_call(
            xs, p["wm"], p["wg"], *tabs_s, p["lnv_g"], p["lnv_b"], p["ws_row"], p["bs_row"],
            tm=tm_s, chunked=False)
        place = lambda q: jnp.einsum("bkgd,kj->bkgjd", q.astype(f32).reshape(db, N_KV_HEADS, GROUP, HEAD_DIM),
                                     eye).reshape(db, N_HEADS, KW).astype(MXU_DTYPE)
        ik_new = ikw[:, 0:IDX_DIM]
        o8a, o8c = _decode_call(
            page_table, cache_a, cache_c, cache_i, l, place(qa), place(qc),
            iq.reshape(db, IDX_HEADS, IDX_DIM), ikw[:, IDX_DIM:IDX_DIM + IDX_HEADS].reshape(db, IDX_HEADS, 1),
            kva.reshape(db, 1, 2 * KW), kvc.reshape(db, 1, 2 * KW), ik_new.reshape(db, 1, IDX_DIM))
        unplace = lambda o: jnp.einsum("bkgkd->bkgd", o.reshape(db, N_KV_HEADS, GROUP, N_KV_HEADS, HEAD_DIM)
                                       ).reshape(db, QW).astype(MXU_DTYPE)
        x1 = _merge_call(xs, unplace(o8a), ob, unplace(o8c), p["wgate"], p["wpa"], p["wpb"], p["wpc"], p["wout"],
                         p["ln1_g"], p["ln1_b"], tm=tm_s)
        xs = _moe_call(x1, p["wrh"], p["wrl"], p["br"], p["w1"], p["w3"], p["w2"], p["ln2_g"], p["ln2_b"], tm=tm_s)
        outs[3].append(kva.reshape(db, 1, 2, N_KV_HEADS, HEAD_DIM))
        outs[4].append(kvc.reshape(db, 1, 2, N_KV_HEADS, HEAD_DIM))
        outs[5].append(ik_new.reshape(db, 1, IDX_DIM))
        outs[6].append(vv.reshape(db, 1, GMLP_WIDTH))

    return (xp.reshape(bp, tp, d), xs.reshape(db, 1, d)) + tuple(jnp.stack(o) for o in outs)
```

```python
import functools
import math

import jax
import jax.numpy as jnp
from jax import lax
from jax.experimental import pallas as pl
from jax.experimental.pallas import tpu as pltpu

HEAD_DIM = 64
N_HEADS = 8
N_KV_HEADS = 4
GROUP = N_HEADS // N_KV_HEADS
MOBA_BLOCK = 256
MOBA_TOPK = 3
IDX_HEADS = 8
IDX_DIM = 32
DSA_TOPK = 256
GMLP_CHUNK = 128
GMLP_GROUPS = 8
GMLP_WIDTH = 512
BRANCH_WIDTH = 512
N_GROUPS = 4
EXPERTS_PER_GROUP = 4
N_EXPERTS = N_GROUPS * EXPERTS_PER_GROUP
ROPE_THETA = 500000.0
ROPE_FRACTION = 4
LN_EPS = 1e-5
NEG_INF = -1e30
DEEPNORM_ALPHA = (2 * 4) ** 0.25

LANES = 128
KEY_TILE = 256
MXU_DTYPE = jnp.bfloat16
VMEM_LIMIT = 56 * 1024 * 1024

QW = N_HEADS * HEAD_DIM
KW = N_KV_HEADS * HEAD_DIM
IQW = IDX_HEADS * IDX_DIM
INT_MIN = -2147483648
HALF_NEG_KEY = -1892283083

f32 = jnp.float32


def _nt_dot(a, b):
    return lax.dot_general(a, b, (((1,), (1,)), ((), ())), preferred_element_type=f32)


def _dot(a, b):
    return jnp.dot(a, b, preferred_element_type=f32)


def _layer_norm(y, g, b):
    mu = jnp.mean(y, axis=-1, keepdims=True)
    d = y - mu
    var = jnp.mean(d * d, axis=-1, keepdims=True)
    return d * lax.rsqrt(var + LN_EPS) * g + b


def _gelu_tanh(x):
    cdf = 0.5 * (1.0 + jnp.tanh(math.sqrt(2.0 / math.pi) * (x + 0.044715 * (x * x * x))))
    return x * cdf


def _sigmoid(x):
    return 1.0 / (1.0 + jnp.exp(-x))


def _sortable(x):
    bits = pltpu.bitcast(x, jnp.int32)
    return jnp.where(bits < 0, bits ^ jnp.int32(0x7FFFFFFF), bits)


def _const_spec(shape):
    nd = len(shape)
    return pl.BlockSpec(shape, lambda *_: (0,) * nd, pipeline_mode=pl.Buffered(1))


def _rope_chunk(h, c, sa, sb, half):
    return h * c + pltpu.roll(h, half, 1) * sa + pltpu.roll(h, LANES - half, 1) * sb


def _proj_body(x_ref, wm_ref, wg_ref, t64_ref, t32_ref, tl_ref, lng_ref, lnb_ref, ws_ref, bs_ref, *out_refs, chunked):
    qa_ref, qc_ref, kva_ref, kvc_ref, iq_ref, ikw_ref, ob_ref, vv_ref = out_refs[:8]
    if chunked:
        kab_ref, kcb_ref, vta_ref, vtc_ref, ikr_ref = out_refs[8:]
    xb = x_ref[...].astype(MXU_DTYPE)
    tm = xb.shape[0]
    c64, sa64, sb64 = t64_ref[0], t64_ref[1], t64_ref[2]
    c32, sa32, sb32 = t32_ref[0], t32_ref[1], t32_ref[2]
    cl, sal, sbl = tl_ref[0], tl_ref[1], tl_ref[2]
    h64 = (HEAD_DIM // ROPE_FRACTION) // 2
    h32 = (IDX_DIM // ROPE_FRACTION) // 2
    scale = HEAD_DIM ** -0.5 * math.log2(math.e)

    hq = _dot(xb, wm_ref[:, 0:2 * QW])
    for c in range(2 * QW // LANES):
        r = _rope_chunk(hq[:, c * LANES:(c + 1) * LANES], c64, sa64, sb64, h64) * scale
        dst = qa_ref if c < QW // LANES else qc_ref
        cc = c % (QW // LANES)
        dst[:, cc * LANES:(cc + 1) * LANES] = r.astype(dst.dtype)

    o = 2 * QW
    hk = _dot(xb, wm_ref[:, o:o + 2 * KW])
    hv = _dot(xb, wm_ref[:, o + 2 * KW:o + 4 * KW])
    for c in range(2 * KW // LANES):
        r = _rope_chunk(hk[:, c * LANES:(c + 1) * LANES], c64, sa64, sb64, h64)
        v = hv[:, c * LANES:(c + 1) * LANES]
        first = c < KW // LANES
        dst = kva_ref if first else kvc_ref
        cc = c % (KW // LANES)
        dst[:, cc * LANES:(cc + 1) * LANES] = r
        dst[:, KW + cc * LANES:KW + (cc + 1) * LANES] = v
        if chunked:
            kb_ref, vt_ref = (kab_ref, vta_ref) if first else (kcb_ref, vtc_ref)
            kb_ref[:, cc * LANES:(cc + 1) * LANES] = r.astype(kb_ref.dtype)
            for j in range(tm // KEY_TILE):
                vt_ref[j, cc * LANES:(cc + 1) * LANES, :] = v[j * KEY_TILE:(j + 1) * KEY_TILE, :].T.astype(vt_ref.dtype)

    o = 2 * QW + 4 * KW
    hi = _dot(xb, wm_ref[:, o:o + IQW + LANES])
    for c in range(IQW // LANES):
        r = _rope_chunk(hi[:, c * LANES:(c + 1) * LANES], c32, sa32, sb32, h32)
        iq_ref[:, c * LANES:(c + 1) * LANES] = r.astype(iq_ref.dtype)
    last = _rope_chunk(hi[:, IQW:IQW + LANES], cl, sal, sbl, h32)
    ikw_ref[...] = last
    if chunked:
        lane = lax.broadcasted_iota(jnp.int32, (tm, LANES), 1)
        t = jnp.where(lane < IDX_DIM, last, 0.0)
        t = t + pltpu.roll(t, IDX_DIM, 1)
        t = t + pltpu.roll(t, 2 * IDX_DIM, 1)
        for c in range(IQW // LANES):
            ikr_ref[:, c * LANES:(c + 1) * LANES] = t.astype(ikr_ref.dtype)

    hg = _dot(xb, wg_ref[...])
    u = _gelu_tanh(hg[:, 0:GMLP_WIDTH])
    vv = _layer_norm(_gelu_tanh(hg[:, GMLP_WIDTH:2 * GMLP_WIDTH]), lng_ref[...], lnb_ref[...])
    vv_ref[...] = vv
    if chunked:
        vb = vv.astype(MXU_DTYPE)
        gd = GMLP_WIDTH // GMLP_GROUPS
        lane_c = lax.broadcasted_iota(jnp.int32, (GMLP_CHUNK, LANES), 1)
        for c in range(tm // GMLP_CHUNK):
            rows = slice(c * GMLP_CHUNK, (c + 1) * GMLP_CHUNK)
            for pr in range(GMLP_WIDTH // LANES):
                vp = vb[rows, pr * LANES:(pr + 1) * LANES]
                g0 = (pr * LANES) // gd
                r0 = _dot(ws_ref[g0], vp)
                r1 = _dot(ws_ref[g0 + 1], vp)
                s = jnp.where(lane_c < gd, r0, r1) + bs_ref[:, pr * LANES:(pr + 1) * LANES]
                ob_ref[rows, pr * LANES:(pr + 1) * LANES] = (u[rows, pr * LANES:(pr + 1) * LANES] * s).astype(ob_ref.dtype)
    else:
        ob_ref[...] = (u * (vv * ws_ref[...] + bs_ref[...])).astype(ob_ref.dtype)


def _proj_call(x, wm, wg, t64, t32, tl, lng, lnb, ws, bs, *, tm, chunked):
    n, d = x.shape
    nt = t64.shape[1] // tm
    grid = (n // tm,)
    row = lambda w: pl.BlockSpec((tm, w), lambda i: (i, 0))
    tab = pl.BlockSpec((3, tm, LANES), lambda i: (0, i % nt, 0))
    outs = [
        (QW, MXU_DTYPE), (QW, MXU_DTYPE),
        (2 * KW, f32), (2 * KW, f32),
        (IQW, MXU_DTYPE),
        (LANES, f32),
        (GMLP_WIDTH, MXU_DTYPE),
        (GMLP_WIDTH, f32),
    ]
    out_specs = [row(w) for w, _ in outs]
    out_shape = [jax.ShapeDtypeStruct((n, w), dt) for w, dt in outs]
    if chunked:
        assert tm % KEY_TILE == 0
        vt_spec = pl.BlockSpec((tm // KEY_TILE, KW, KEY_TILE), lambda i: (i, 0, 0))
        vt_shape = jax.ShapeDtypeStruct((n // KEY_TILE, KW, KEY_TILE), MXU_DTYPE)
        out_specs += [row(KW), row(KW), vt_spec, vt_spec, row(IQW)]
        out_shape += [jax.ShapeDtypeStruct((n, KW), MXU_DTYPE)] * 2 + [vt_shape] * 2 \
            + [jax.ShapeDtypeStruct((n, IQW), MXU_DTYPE)]
    return pl.pallas_call(
        functools.partial(_proj_body, chunked=chunked),
        grid=grid,
        in_specs=[row(d), _const_spec(wm.shape), _const_spec(wg.shape), tab, tab, tab,
                  _const_spec(lng.shape), _const_spec(lnb.shape), _const_spec(ws.shape), _const_spec(bs.shape)],
        out_specs=out_specs,
        out_shape=out_shape,
        compiler_params=pltpu.CompilerParams(dimension_semantics=("parallel",), vmem_limit_bytes=VMEM_LIMIT),
        name="proj",
    )(x, wm, wg, t64, t32, tl, lng, lnb, ws, bs)


def _head_rows(q_ref, kvh, tq):
    qp = q_ref[:, kvh * LANES:(kvh + 1) * LANES].astype(f32)
    rl = pltpu.roll(qp, HEAD_DIM, 1)
    lo = lax.broadcasted_iota(jnp.int32, (tq, LANES), 1) < HEAD_DIM
    if kvh % 2 == 0:
        g0, g1 = jnp.where(lo, qp, 0.0), jnp.where(lo, rl, 0.0)
    else:
        g0, g1 = jnp.where(lo, 0.0, rl), jnp.where(lo, 0.0, qp)
    return jnp.concatenate([g0, g1], axis=0).astype(MXU_DTYPE)


ACC_ROWS = HEAD_DIM + 16


def _value_rows(vt_ref, kt, kvh):
    vt = vt_ref[kt, kvh * HEAD_DIM:(kvh + 1) * HEAD_DIM, :]
    return jnp.concatenate([vt, jnp.ones((ACC_ROWS - HEAD_DIM, vt.shape[1]), vt.dtype)], axis=0)


def _attend_tile(k_ref, vt_ref, qh, acc_sc, s_sc, state, kt, masks):
    start = pl.multiple_of(kt * KEY_TILE, KEY_TILE)
    for kvh in range(N_KV_HEADS):
        pc = kvh // 2
        s_sc[kvh] = _nt_dot(k_ref[pl.ds(start, KEY_TILE), pc * LANES:(pc + 1) * LANES], qh[kvh])
    out = []
    for kvh in range(N_KV_HEADS):
        m, acc = _softmax_step(s_sc[kvh], masks[kvh], state[kvh], acc_sc[kvh], _value_rows(vt_ref, kt, kvh))
        acc_sc[kvh] = acc
        out.append(m)
    return tuple(out)


def _head_out(acc, tq):
    o = acc[0:HEAD_DIM] / acc[HEAD_DIM:HEAD_DIM + 1]
    return jnp.concatenate([o[:, g * tq:(g + 1) * tq] for g in range(GROUP)], axis=0).T


def _softmax_step(s, mask, m, acc, vt):
    s = jnp.where(mask, s, NEG_INF)
    m_new = jnp.maximum(m, jnp.max(s, axis=0, keepdims=True))
    p = jnp.exp2(s - m_new)
    acc_new = jnp.exp2(m - m_new) * acc + _dot(vt, p.astype(MXU_DTYPE))
    return m_new, acc_new


def _moba_body(q_ref, k_ref, vt_ref, o_ref, kmean_sc, sel_sc, acc_sc, s_sc, *, tq):
    i = pl.program_id(1)
    nb = k_ref.shape[0] // MOBA_BLOCK
    nbp = kmean_sc.shape[0]
    cols = GROUP * tq

    @pl.when(i == 0)
    def _():
        kmean_sc[...] = jnp.zeros_like(kmean_sc)
        for n in range(nb):
            blk = k_ref[n * MOBA_BLOCK:(n + 1) * MOBA_BLOCK, :].astype(f32)
            kmean_sc[n:n + 1, :] = jnp.mean(blk, axis=0, keepdims=True)

    blk_id = lax.broadcasted_iota(jnp.int32, (nbp, cols), 0)
    qh = [_head_rows(q_ref, kvh, tq) for kvh in range(N_KV_HEADS)]

    for kvh in range(N_KV_HEADS):
        pc = kvh // 2
        gate = _nt_dot(kmean_sc[:, pc * LANES:(pc + 1) * LANES].astype(MXU_DTYPE), qh[kvh])
        g = jnp.where(blk_id < i, gate, NEG_INF)
        sel = jnp.zeros((nbp, cols), f32)
        for _ in range(MOBA_TOPK):
            mx = jnp.max(g, axis=0, keepdims=True)
            first = jnp.min(jnp.where(g == mx, blk_id, 1 << 20), axis=0, keepdims=True)
            hit = blk_id == first
            sel = jnp.where(hit & (mx > 0.5 * NEG_INF), 1.0, sel)
            g = jnp.where(hit, -3e38, g)
        sel_sc[kvh] = sel

    acc_sc[...] = jnp.zeros_like(acc_sc)
    key_in_blk = lax.broadcasted_iota(jnp.int32, (KEY_TILE, cols), 0)
    q_in_blk = lax.broadcasted_iota(jnp.int32, (KEY_TILE, cols), 1) % tq
    state = tuple(jnp.full((1, cols), NEG_INF, f32) for _ in range(N_KV_HEADS))
    state = _attend_tile(k_ref, vt_ref, qh, acc_sc, s_sc, state, i, [key_in_blk <= q_in_blk] * N_KV_HEADS)

    def past(n, state):
        chosen = [jnp.broadcast_to(sel_sc[kvh, pl.ds(n, 1), :] > 0.5, (KEY_TILE, cols)) for kvh in range(N_KV_HEADS)]
        return _attend_tile(k_ref, vt_ref, qh, acc_sc, s_sc, state, n, chosen)

    lax.fori_loop(0, i, past, state)
    for kvh in range(N_KV_HEADS):
        o_ref[:, kvh * LANES:(kvh + 1) * LANES] = _head_out(acc_sc[kvh], tq).astype(o_ref.dtype)


def _moba_call(q, kb, vt, *, batch, t):
    tq = MOBA_BLOCK
    assert tq == KEY_TILE and GROUP * HEAD_DIM == LANES
    nq = t // tq
    nbp = -(-nq // 16) * 16
    return pl.pallas_call(
        functools.partial(_moba_body, tq=tq),
        grid=(batch, nq),
        in_specs=[pl.BlockSpec((tq, QW), lambda b, i: (b * nq + i, 0)),
                  pl.BlockSpec((t, KW), lambda b, i: (b, 0)),
                  pl.BlockSpec((nq, KW, KEY_TILE), lambda b, i: (b, 0, 0))],
        out_specs=pl.BlockSpec((tq, QW), lambda b, i: (b * nq + i, 0)),
        out_shape=jax.ShapeDtypeStruct(q.shape, MXU_DTYPE),
        scratch_shapes=[pltpu.VMEM((nbp, KW), f32), pltpu.VMEM((N_KV_HEADS, nbp, GROUP * tq), f32),
                        pltpu.VMEM((N_KV_HEADS, ACC_ROWS, GROUP * tq), f32),
                        pltpu.VMEM((N_KV_HEADS, KEY_TILE, GROUP * tq), f32)],
        compiler_params=pltpu.CompilerParams(dimension_semantics=("parallel", "arbitrary"),
                                             vmem_limit_bytes=VMEM_LIMIT),
        name="moba",
    )(q, kb, vt)


def _select_threshold(count_ge, n_top, rows):
    def step(it, t_u):
        cand_u = t_u | lax.shift_left(jnp.int32(1), jnp.int32(31) - it)
        cnt = count_ge(cand_u ^ jnp.int32(INT_MIN))
        return jnp.where(cnt >= n_top, cand_u, t_u)
    t_u = lax.fori_loop(0, 32, step, jnp.zeros((1, rows), jnp.int32))
    return t_u ^ jnp.int32(INT_MIN)


def _dsa_body(iq_ref, ikw_ref, ikr_ref, q_ref, k_ref, vt_ref, o_ref, key_sc, cut_sc, acc_sc, s_sc,
              *, tq, n_top, idx_bits):
    i = pl.program_id(1)
    ntile = i + 1
    cols = GROUP * tq
    lane_q = lax.broadcasted_iota(jnp.int32, (tq, IQW), 1)
    krow = lax.broadcasted_iota(jnp.int32, (tq, tq), 0)
    qcol = lax.broadcasted_iota(jnp.int32, (tq, tq), 1)
    half_neg = jnp.int32(HALF_NEG_KEY)

    iqv = iq_ref[...]
    iqm = [jnp.where((lane_q >= h * IDX_DIM) & (lane_q < (h + 1) * IDX_DIM), iqv, jnp.zeros_like(iqv))
           for h in range(IDX_HEADS)]
    iwt = ikw_ref[...].T
    iw = [iwt[IDX_DIM + h:IDX_DIM + h + 1, :] for h in range(IDX_HEADS)]
    sc_scale = (IDX_DIM ** -0.5) * (IDX_HEADS ** -0.5)

    def score_tile(kt, _):
        ik_t = ikr_ref[pl.ds(pl.multiple_of(kt * tq, tq), tq), :]
        sc = jnp.zeros((tq, tq), f32)
        for h in range(IDX_HEADS):
            sc = sc + iw[h] * jnp.maximum(_nt_dot(ik_t, iqm[h]), 0.0)
        sc = sc * sc_scale
        sc = jnp.where(kt * tq + krow <= i * tq + qcol, sc, NEG_INF)
        key_sc[kt] = _sortable(sc)
        return 0

    lax.fori_loop(0, ntile, score_tile, 0)

    def count_where(pred):
        nacc = 4

        def body(kt, cs):
            hit = jnp.where(pred(key_sc[kt], kt), 1, 0).reshape(nacc, tq // (8 * nacc), 8, tq)
            return tuple(c + jnp.sum(hit[a], axis=0) for a, c in enumerate(cs))
        cs = lax.fori_loop(0, ntile, body, tuple(jnp.zeros((8, tq), jnp.int32) for _ in range(nacc)))
        return jnp.sum(sum(cs[1:], cs[0]), axis=0, keepdims=True)

    thr = _select_threshold(lambda cand: count_where(lambda k, kt: k >= cand), n_top, tq)
    cnt_gt = count_where(lambda k, kt: k > thr)
    cnt_eq = count_where(lambda k, kt: k == thr)
    need = n_top - cnt_gt

    cut_sc[...] = jnp.full(cut_sc.shape, 1 << 30, jnp.int32)
    excess = (cnt_eq > need) & (thr > half_neg)

    @pl.when(jnp.max(jnp.where(excess, 1, 0)) > 0)
    def _():
        def step(it, t_i):
            cand = t_i | lax.shift_left(jnp.int32(1), jnp.int32(idx_bits - 1) - it)
            cnt = count_where(lambda k, kt: (k == thr) & (kt * tq + krow < cand))
            return jnp.where(cnt < need, cand, t_i)
        t_i = lax.fori_loop(0, idx_bits, step, jnp.zeros((1, tq), jnp.int32))
        cut_sc[...] = jnp.broadcast_to(jnp.where(excess, t_i, 1 << 30), cut_sc.shape)

    cut = cut_sc[0:1, :]

    qh = [_head_rows(q_ref, kvh, tq) for kvh in range(N_KV_HEADS)]
    acc_sc[...] = jnp.zeros_like(acc_sc)

    def attend(kt, state):
        k = key_sc[kt]
        keep = ((k > thr) | ((k == thr) & (kt * tq + krow <= cut))) & (k > half_neg)
        mask = jnp.concatenate([keep] * GROUP, axis=1)
        return _attend_tile(k_ref, vt_ref, qh, acc_sc, s_sc, state, kt, [mask] * N_KV_HEADS)

    lax.fori_loop(0, ntile, attend, tuple(jnp.full((1, cols), NEG_INF, f32) for _ in range(N_KV_HEADS)))
    for kvh in range(N_KV_HEADS):
        o_ref[:, kvh * LANES:(kvh + 1) * LANES] = _head_out(acc_sc[kvh], tq).astype(o_ref.dtype)


def _dsa_call(iq, ikw, ikr, q, kb, vt, *, batch, t):
    tq = KEY_TILE
    nq = t // tq
    n_top = min(DSA_TOPK, t // 4)
    assert n_top <= tq and t % tq == 0 and GROUP * HEAD_DIM == LANES
    rowspec = lambda w: pl.BlockSpec((tq, w), lambda b, i: (b * nq + i, 0))
    seqspec = lambda w: pl.BlockSpec((t, w), lambda b, i: (b, 0))
    return pl.pallas_call(
        functools.partial(_dsa_body, tq=tq, n_top=n_top, idx_bits=max(1, (t - 1).bit_length())),
        grid=(batch, nq),
        in_specs=[rowspec(IQW), rowspec(LANES), seqspec(IQW), rowspec(QW), seqspec(KW),
                  pl.BlockSpec((nq, KW, KEY_TILE), lambda b, i: (b, 0, 0))],
        out_specs=rowspec(QW),
        out_shape=jax.ShapeDtypeStruct(q.shape, MXU_DTYPE),
        scratch_shapes=[pltpu.VMEM((nq, tq, tq), jnp.int32), pltpu.VMEM((8, tq), jnp.int32),
                        pltpu.VMEM((N_KV_HEADS, ACC_ROWS, GROUP * tq), f32),
                        pltpu.VMEM((N_KV_HEADS, KEY_TILE, GROUP * tq), f32)],
        compiler_params=pltpu.CompilerParams(dimension_semantics=("parallel", "arbitrary"),
                                             vmem_limit_bytes=VMEM_LIMIT),
        name="dsa",
    )(iq, ikw, ikr, q, kb, vt)


RADIX_BITS = 4


def _radix_select(count_fn, total_bits, top_bit):
    digit = lax.broadcasted_iota(jnp.int32, (1 << RADIX_BITS, 1), 0)
    t = jnp.zeros((1, 1), jnp.int32)
    for step in range(total_bits // RADIX_BITS):
        shift = top_bit + 1 - RADIX_BITS * (step + 1)
        ok = count_fn(t | lax.shift_left(digit, jnp.int32(shift)))
        best = jnp.sum(jnp.where(ok, 1, 0), axis=0, keepdims=True) - 1
        t = t | lax.shift_left(best, jnp.int32(shift))
    return t


def _decode_seq(abuf, cbuf, ibuf, qa_ref, qc_ref, iq_ref, iw_ref, na_ref, nc_ref, ni_ref, oa_ref, oc_ref,
                *, n_pages, page, n_top):
    past = n_pages * page
    pages_per_block = MOBA_BLOCK // page
    nbp = n_pages // pages_per_block
    lane = lax.broadcasted_iota(jnp.int32, (N_HEADS, LANES), 1)

    def finish(s_list, mask_list, q8, knew, vnew, vbuf, extra_ok):
        s_new = jnp.sum(q8.astype(f32) * knew, axis=1, keepdims=True)
        s_new = jnp.where(extra_ok, s_new, NEG_INF)
        sm = [jnp.where(mk, s, NEG_INF) for s, mk in zip(s_list, mask_list)]
        m = s_new
        for s in sm:
            m = jnp.maximum(m, jnp.max(s, axis=1, keepdims=True))
        p_new = jnp.where(extra_ok, jnp.exp2(s_new - m), 0.0)
        l = p_new
        acc = p_new * vnew
        for p, (s, mk) in enumerate(zip(sm, mask_list)):
            pc = jnp.where(mk, jnp.exp2(s - m), 0.0)
            l = l + jnp.sum(pc, axis=1, keepdims=True)
            acc = acc + _nt_dot(pc.astype(MXU_DTYPE), vbuf[p, KW:2 * KW, :].astype(MXU_DTYPE))
        return acc / l

    q8 = qa_ref[...]
    sa = [_dot(q8, abuf[p, 0:KW, :].astype(MXU_DTYPE)) for p in range(n_pages)]
    gate = jnp.full((N_HEADS, LANES), NEG_INF, f32)
    for n in range(nbp):
        tot = sum(jnp.sum(sa[n * pages_per_block + j], axis=1, keepdims=True) for j in range(pages_per_block))
        gate = jnp.where(lane == n, tot * (1.0 / MOBA_BLOCK), gate)
    sel = jnp.zeros((N_HEADS, LANES), f32)
    g = gate
    for _ in range(min(MOBA_TOPK, nbp + 1)):
        mx = jnp.max(g, axis=1, keepdims=True)
        first = jnp.min(jnp.where(g == mx, lane, 1 << 20), axis=1, keepdims=True)
        hit = lane == first
        sel = jnp.where(hit & (mx > 0.5 * NEG_INF), 1.0, sel)
        g = jnp.where(hit, -3e38, g)
    chosen = [jnp.max(jnp.where(lane == n, sel, 0.0), axis=1, keepdims=True) > 0.5 for n in range(nbp)]
    masks = [jnp.broadcast_to(chosen[p // pages_per_block], sa[p].shape) for p in range(n_pages)]
    oa_ref[...] = finish(sa, masks, q8, na_ref[:, 0:KW], na_ref[:, KW:2 * KW], abuf, True)

    sc_scale = (IDX_DIM ** -0.5) * (IDX_HEADS ** -0.5)
    iq8 = iq_ref[...]
    iw8 = iw_ref[...]
    sc = jnp.concatenate(
        [jnp.sum(iw8 * jnp.maximum(_dot(iq8, ibuf[p].astype(MXU_DTYPE)), 0.0), axis=0, keepdims=True)
         for p in range(n_pages)], axis=1) * sc_scale
    rel_n = jnp.maximum(jnp.sum(iq8.astype(f32) * ni_ref[...], axis=1, keepdims=True), 0.0)
    sc_n = jnp.sum(iw8 * rel_n, axis=0, keepdims=True) * sc_scale
    key = _sortable(sc)
    key_n = _sortable(sc_n)
    idx = lax.broadcasted_iota(jnp.int32, (1, past), 1)
    nd = 1 << RADIX_BITS
    key_b = jnp.broadcast_to(key, (nd, past))
    idx_b = jnp.broadcast_to(idx, (nd, past))

    def count(pred_past, pred_new):
        return jnp.sum(jnp.where(pred_past, 1, 0), axis=1, keepdims=True) + jnp.where(pred_new, 1, 0)

    def enough(cand_u):
        cand = cand_u ^ jnp.int32(INT_MIN)
        return count(key_b >= cand, key_n >= cand) >= n_top

    thr = _radix_select(enough, 32, 31) ^ jnp.int32(INT_MIN)
    need = n_top - count(key > thr, key_n > thr)
    eq = key == thr
    eq_b = jnp.broadcast_to(eq, (nd, past))
    idx_bits = -(-max(1, past.bit_length()) // RADIX_BITS) * RADIX_BITS
    cut = _radix_select(lambda cand: jnp.sum(jnp.where(eq_b & (idx_b < cand), 1, 0), axis=1, keepdims=True) < need,
                        idx_bits, idx_bits - 1)
    eq_past = jnp.sum(jnp.where(eq, 1, 0), axis=1, keepdims=True)
    keep = (key > thr) | (eq & (idx <= cut))
    keep_n = (key_n > thr) | ((key_n == thr) & (eq_past < need))
    q8c = qc_ref[...]
    s_c = [_dot(q8c, cbuf[p, 0:KW, :].astype(MXU_DTYPE)) for p in range(n_pages)]
    masks_c = [jnp.broadcast_to(keep[:, p * page:(p + 1) * page], s_c[p].shape) for p in range(n_pages)]
    oc_ref[...] = finish(s_c, masks_c, q8c, nc_ref[:, 0:KW], nc_ref[:, KW:2 * KW], cbuf, keep_n)


def _decode_body(pt_ref, ca_hbm, cc_hbm, ci_hbm, qa_ref, qc_ref, iq_ref, iw_ref, na_ref, nc_ref, ni_ref,
                 oa_ref, oc_ref, abuf, cbuf, ibuf, sems, *, layer, seqs, n_pages, page, n_top):
    b = pl.program_id(0)
    slot = lax.rem(b, 2)

    def page_copies(step, sl):
        out = []
        for j in range(seqs):
            for p in range(n_pages):
                pg = pt_ref[step * seqs + j, p]
                out.append(pltpu.make_async_copy(ca_hbm.at[layer, pg], abuf.at[sl, j, p], sems.at[0, sl]))
                out.append(pltpu.make_async_copy(cc_hbm.at[layer, pg], cbuf.at[sl, j, p], sems.at[1, sl]))
                out.append(pltpu.make_async_copy(ci_hbm.at[layer, pg], ibuf.at[sl, j, p], sems.at[2, sl]))
        return out

    @pl.when(b == 0)
    def _():
        for c in page_copies(0, 0):
            c.start()

    @pl.when(b + 1 < pl.num_programs(0))
    def _():
        for c in page_copies(b + 1, 1 - slot):
            c.start()

    for c in page_copies(b, slot):
        c.wait()

    for j in range(seqs):
        _decode_seq(abuf.at[slot, j], cbuf.at[slot, j], ibuf.at[slot, j], qa_ref.at[j], qc_ref.at[j], iq_ref.at[j],
                    iw_ref.at[j], na_ref.at[j], nc_ref.at[j], ni_ref.at[j], oa_ref.at[j], oc_ref.at[j],
                    n_pages=n_pages, page=page, n_top=n_top)


def _decode_call(page_table, cache_at, cache_ct, cache_it, layer, q8a, q8c, iq8, iw8, kva, kvc, ik):
    db, n_pages = page_table.shape
    page = cache_at.shape[3]
    past = n_pages * page
    assert past % MOBA_BLOCK == 0 and MOBA_BLOCK % page == 0 and page == LANES
    n_top = min(DSA_TOPK, (past + 1) // 4)
    assert n_top <= past
    seqs = 2 if db % 2 == 0 else 1
    seqspec = lambda r, w: pl.BlockSpec((seqs, r, w), lambda b, pt: (b, 0, 0))
    hbm = pl.BlockSpec(memory_space=pl.ANY)
    grid_spec = pltpu.PrefetchScalarGridSpec(
        num_scalar_prefetch=1,
        grid=(db // seqs,),
        in_specs=[hbm, hbm, hbm,
                  seqspec(N_HEADS, KW), seqspec(N_HEADS, KW), seqspec(IDX_HEADS, IDX_DIM), seqspec(IDX_HEADS, 1),
                  seqspec(1, 2 * KW), seqspec(1, 2 * KW), seqspec(1, IDX_DIM)],
        out_specs=[seqspec(N_HEADS, KW), seqspec(N_HEADS, KW)],
        scratch_shapes=[pltpu.VMEM((2, seqs, n_pages, 2 * KW, page), f32),
                        pltpu.VMEM((2, seqs, n_pages, 2 * KW, page), f32),
                        pltpu.VMEM((2, seqs, n_pages, IDX_DIM, page), f32), pltpu.SemaphoreType.DMA((3, 2))],
    )
    return pl.pallas_call(
        functools.partial(_decode_body, layer=layer, seqs=seqs, n_pages=n_pages, page=page, n_top=n_top),
        grid_spec=grid_spec,
        out_shape=[jax.ShapeDtypeStruct((db, N_HEADS, KW), f32)] * 2,
        compiler_params=pltpu.CompilerParams(dimension_semantics=("arbitrary",), vmem_limit_bytes=VMEM_LIMIT),
        name="decode",
    )(page_table, cache_at, cache_ct, cache_it, q8a, q8c, iq8, iw8, kva, kvc, ik)


def _merge_body(x_ref, oa_ref, ob_ref, oc_ref, wgate_ref, wpa_ref, wpb_ref, wpc_ref, wout_ref, g_ref, b_ref, o_ref):
    x = x_ref[...]
    d = x.shape[1]
    xb = x.astype(MXU_DTYPE)
    merged = None
    for k, (o_r, wp_r) in enumerate(((oa_ref, wpa_ref), (ob_ref, wpb_ref), (oc_ref, wpc_ref))):
        gate = _sigmoid(_dot(xb, wgate_ref[:, k * d:(k + 1) * d]))
        term = gate * _dot(o_r[...].astype(MXU_DTYPE), wp_r[...])
        merged = term if merged is None else merged + term
    y = DEEPNORM_ALPHA * x + _dot(merged.astype(MXU_DTYPE), wout_ref[...])
    o_ref[...] = _layer_norm(y, g_ref[...], b_ref[...])


def _merge_call(x, oa, ob, oc, wgate, wpa, wpb, wpc, wout, g, b, *, tm):
    n, d = x.shape
    row = lambda w: pl.BlockSpec((tm, w), lambda i: (i, 0))
    return pl.pallas_call(
        _merge_body,
        grid=(n // tm,),
        in_specs=[row(d), row(BRANCH_WIDTH), row(BRANCH_WIDTH), row(BRANCH_WIDTH)]
        + [_const_spec(a.shape) for a in (wgate, wpa, wpb, wpc, wout, g, b)],
        out_specs=row(d),
        out_shape=jax.ShapeDtypeStruct((n, d), f32),
        compiler_params=pltpu.CompilerParams(dimension_semantics=("parallel",), vmem_limit_bytes=VMEM_LIMIT),
        name="merge",
    )(x, oa, ob, oc, wgate, wpa, wpb, wpc, wout, g, b)


def _moe_body(x_ref, wrh_ref, wrl_ref, br_ref, w1_ref, w3_ref, w2_ref, g_ref, b_ref, o_ref, hh_sc):
    x = x_ref[...]
    tm = x.shape[0]
    xb = x.astype(MXU_DTYPE)
    xl = (x - xb.astype(f32)).astype(MXU_DTYPE)
    logits = _dot(xb, wrh_ref[...]) + _dot(xb, wrl_ref[...]) + _dot(xl, wrh_ref[...]) + br_ref[...]
    lane = lax.broadcasted_iota(jnp.int32, (tm, LANES), 1)
    big = 1 << 20
    gl = jnp.where(lane < N_GROUPS, logits, -jnp.inf)
    gmax = jnp.max(gl, axis=1, keepdims=True)
    g_sel = jnp.min(jnp.where(gl == gmax, lane, big), axis=1, keepdims=True)
    g_w = 1.0 / jnp.sum(jnp.exp(gl - gmax), axis=1, keepdims=True)
    lo = N_GROUPS + EXPERTS_PER_GROUP * g_sel
    el = jnp.where((lane >= lo) & (lane < lo + EXPERTS_PER_GROUP), logits, -jnp.inf)
    m1 = jnp.max(el, axis=1, keepdims=True)
    i1 = jnp.min(jnp.where(el == m1, lane, big), axis=1, keepdims=True)
    el2 = jnp.where(lane == i1, -jnp.inf, el)
    m2 = jnp.max(el2, axis=1, keepdims=True)
    i2 = jnp.min(jnp.where(el2 == m2, lane, big), axis=1, keepdims=True)
    e2 = jnp.exp(m2 - m1)
    w_first = g_w / (1.0 + e2)
    w_second = g_w * e2 / (1.0 + e2)
    comb = jnp.where(lane == i1, w_first, 0.0) + jnp.where(lane == i2, w_second, 0.0)

    ff = w1_ref.shape[2]
    for e in range(N_EXPERTS):
        hg = _dot(xb, w1_ref[e])
        hu = _dot(xb, w3_ref[e])
        ce = comb[:, N_GROUPS + e:N_GROUPS + e + 1]
        hh_sc[:, e * ff:(e + 1) * ff] = (hg * _sigmoid(hg) * hu * ce).astype(hh_sc.dtype)
    y = DEEPNORM_ALPHA * x + _dot(hh_sc[...], w2_ref[...])
    o_ref[...] = _layer_norm(y, g_ref[...], b_ref[...])


def _moe_call(x, wrh, wrl, br, w1, w3, w2, g, b, *, tm):
    n, d = x.shape
    row = pl.BlockSpec((tm, d), lambda i: (i, 0))
    return pl.pallas_call(
        _moe_body,
        grid=(n // tm,),
        in_specs=[row] + [_const_spec(a.shape) for a in (wrh, wrl, br, w1, w3, w2, g, b)],
        out_specs=row,
        out_shape=jax.ShapeDtypeStruct((n, d), f32),
        scratch_shapes=[pltpu.VMEM((tm, w2.shape[0]), MXU_DTYPE)],
        compiler_params=pltpu.CompilerParams(dimension_semantics=("parallel",), vmem_limit_bytes=VMEM_LIMIT),
        name="moe",
    )(x, wrh, wrl, br, w1, w3, w2, g, b)


def _rope_table(pos, dim, ident_from=None):
    rot = dim // ROPE_FRACTION
    half = rot // 2
    inv_freq = ROPE_THETA ** (-jnp.arange(half, dtype=f32) / half)
    ang = pos.astype(f32)[:, None] * inv_freq[None, :]
    cos, sin = jnp.cos(ang), jnp.sin(ang)
    n = pos.shape[0]
    z = lambda w: jnp.zeros((n, w), f32)
    c = jnp.concatenate([cos, cos, jnp.ones((n, dim - rot), f32)], axis=1)
    sa = jnp.concatenate([z(half), sin, z(dim - rot)], axis=1)
    sb = jnp.concatenate([-sin, z(dim - half)], axis=1)
    tab = jnp.stack([jnp.tile(a, (1, LANES // dim)) for a in (c, sa, sb)])
    if ident_from is not None:
        keep = (jnp.arange(LANES) < ident_from)[None, None, :]
        ident = jnp.stack([jnp.ones((n, LANES), f32), jnp.zeros((n, LANES), f32), jnp.zeros((n, LANES), f32)])
        tab = jnp.where(keep, tab, ident)
    return tab


def _layer_params(l, w_in, w_pa, w_pb, w_pc, w_out, w_s, b_s, ln_v_g, ln_v_b, ln1_g, ln1_b, ln2_g, ln2_b,
                  w_rg, b_rg, w_re, b_re, w_e1, w_e3, w_e2):
    d = w_in.shape[1]
    w = w_in[l]
    offs = {}
    o = 0
    for name, width in (("aq", QW), ("ak", KW), ("av", KW), ("cq", QW), ("ck", KW), ("cv", KW),
                        ("iq", IQW), ("ik", IDX_DIM), ("iw", IDX_HEADS), ("gu", GMLP_WIDTH), ("gv", GMLP_WIDTH),
                        ("ga", d), ("gb", d), ("gc", d)):
        offs[name] = (o, o + width)
        o += width
    assert o == w.shape[1]
    cols = lambda n: w[:, offs[n][0]:offs[n][1]]
    pad = jnp.zeros((d, LANES - IDX_DIM - IDX_HEADS), w.dtype)
    wm = jnp.concatenate([cols("aq"), cols("cq"), cols("ak"), cols("ck"), cols("av"), cols("cv"),
                          cols("iq"), cols("ik"), cols("iw"), pad], axis=1).astype(MXU_DTYPE)
    wg = jnp.concatenate([cols("gu"), cols("gv")], axis=1).astype(MXU_DTYPE)
    wgate = jnp.concatenate([cols("ga"), cols("gb"), cols("gc")], axis=1).astype(MXU_DTYPE)
    causal = jnp.tril(jnp.ones((GMLP_CHUNK, GMLP_CHUNK), dtype=bool))
    gd = GMLP_WIDTH // GMLP_GROUPS
    ws = jnp.where(causal[None], w_s[l], 0)
    wr = jnp.concatenate([w_rg[l], w_re[l], jnp.zeros((d, LANES - N_GROUPS - N_EXPERTS), f32)], axis=1)
    wrh = wr.astype(MXU_DTYPE)
    row = lambda a: a.reshape(1, -1)
    return dict(
        wm=wm, wg=wg, wgate=wgate,
        ws_chunk=ws.astype(MXU_DTYPE),
        bs_chunk=jnp.repeat(b_s[l].T, gd, axis=1),
        ws_row=jnp.repeat(w_s[l][:, 0, 0], gd).reshape(1, -1),
        bs_row=jnp.repeat(b_s[l][:, 0], gd).reshape(1, -1),
        lnv_g=row(ln_v_g[l]), lnv_b=row(ln_v_b[l]),
        wpa=w_pa[l].astype(MXU_DTYPE), wpb=w_pb[l].astype(MXU_DTYPE), wpc=w_pc[l].astype(MXU_DTYPE),
        wout=w_out[l].astype(MXU_DTYPE), ln1_g=row(ln1_g[l]), ln1_b=row(ln1_b[l]),
        wrh=wrh, wrl=(wr - wrh.astype(f32)).astype(MXU_DTYPE),
        br=jnp.concatenate([b_rg[l], b_re[l], jnp.zeros((LANES - N_GROUPS - N_EXPERTS,), f32)]).reshape(1, -1),
        w1=w_e1[l].astype(MXU_DTYPE), w3=w_e3[l].astype(MXU_DTYPE),
        w2=w_e2[l].reshape(-1, d).astype(MXU_DTYPE), ln2_g=row(ln2_g[l]), ln2_b=row(ln2_b[l]),
    )


def kernel(x_prompt, x_sample, cache_a_kv, cache_c_kv, cache_c_kidx, page_table, w_in, w_pa, w_pb, w_pc, w_out, w_s, b_s, ln_v_g, ln_v_b, ln1_g, ln1_b, ln2_g, ln2_b, w_rg, b_rg, w_re, b_re, w_e1, w_e3, w_e2):
    bp, tp, d = x_prompt.shape
    db, ts, _ = x_sample.shape
    depth = w_in.shape[0]
    n_phys, page = cache_a_kv.shape[1], cache_a_kv.shape[2]
    n_pages = page_table.shape[1]
    past = n_pages * page
    assert ts == 1 and tp % MOBA_BLOCK == 0 and cache_a_kv.shape[3:] == (2, N_KV_HEADS, HEAD_DIM)

    cache_a = jnp.transpose(cache_a_kv, (0, 1, 3, 4, 5, 2)).reshape(depth, n_phys, 2 * KW, page)
    cache_c = jnp.transpose(cache_c_kv, (0, 1, 3, 4, 5, 2)).reshape(depth, n_phys, 2 * KW, page)
    cache_i = jnp.transpose(cache_c_kidx, (0, 1, 3, 2))
    page_table = page_table.astype(jnp.int32)

    tm_p = 512 if tp % 512 == 0 else MOBA_BLOCK
    tm_s = db
    pos_p = jnp.arange(tp, dtype=jnp.int32)
    pos_s = jnp.full((db,), past, jnp.int32)
    tabs_p = (_rope_table(pos_p, HEAD_DIM), _rope_table(pos_p, IDX_DIM), _rope_table(pos_p, IDX_DIM, IDX_DIM))
    tabs_s = (_rope_table(pos_s, HEAD_DIM), _rope_table(pos_s, IDX_DIM), _rope_table(pos_s, IDX_DIM, IDX_DIM))

    xp = x_prompt.reshape(bp * tp, d)
    xs = x_sample.reshape(db, d)
    outs = [[] for _ in range(7)]
    eye = jnp.eye(N_KV_HEADS, dtype=f32)
    for l in range(depth):
        p = _layer_params(l, w_in, w_pa, w_pb, w_pc, w_out, w_s, b_s, ln_v_g, ln_v_b, ln1_g, ln1_b, ln2_g, ln2_b,
                          w_rg, b_rg, w_re, b_re, w_e1, w_e3, w_e2)

        qa, qc, kva, kvc, iq, ikw, ob, _, kab, kcb, vta, vtc, ikr = _proj_call(
            xp, p["wm"], p["wg"], *tabs_p, p["lnv_g"], p["lnv_b"], p["ws_chunk"], p["bs_chunk"],
            tm=tm_p, chunked=True)
        oa = _moba_call(qa, kab, vta, batch=bp, t=tp)
        oc = _dsa_call(iq, ikw, ikr, qc, kcb, vtc, batch=bp, t=tp)
        x1 = _merge_call(xp, oa, ob, oc, p["wgate"], p["wpa"], p["wpb"], p["wpc"], p["wout"],
                         p["ln1_g"], p["ln1_b"], tm=tm_p)
        xp = _moe_call(x1, p["wrh"], p["wrl"], p["br"], p["w1"], p["w3"], p["w2"], p["ln2_g"], p["ln2_b"], tm=tm_p)
        outs[0].append(kva.reshape(bp, tp // page, page, 2, N_KV_HEADS, HEAD_DIM))
        outs[1].append(kvc.reshape(bp, tp // page, page, 2, N_KV_HEADS, HEAD_DIM))
        outs[2].append(ikw[:, 0:IDX_DIM].reshape(bp, tp // page, page, IDX_DIM))

        qa, qc, kva, kvc, iq, ikw, ob, vv = _proj_call(
            xs, p["wm"], p["wg"], *tabs_s, p["lnv_g"], p["lnv_b"], p["ws_row"], p["bs_row"],
            tm=tm_s, chunked=False)
        place = lambda q: jnp.einsum("bkgd,kj->bkgjd", q.astype(f32).reshape(db, N_KV_HEADS, GROUP, HEAD_DIM),
                                     eye).reshape(db, N_HEADS, KW).astype(MXU_DTYPE)
        ik_new = ikw[:, 0:IDX_DIM]
        o8a, o8c = _decode_call(
            page_table, cache_a, cache_c, cache_i, l, place(qa), place(qc),
            iq.reshape(db, IDX_HEADS, IDX_DIM), ikw[:, IDX_DIM:IDX_DIM + IDX_HEADS].reshape(db, IDX_HEADS, 1),
            kva.reshape(db, 1, 2 * KW), kvc.reshape(db, 1, 2 * KW), ik_new.reshape(db, 1, IDX_DIM))
        unplace = lambda o: jnp.einsum("bkgkd->bkgd", o.reshape(db, N_KV_HEADS, GROUP, N_KV_HEADS, HEAD_DIM)
                                       ).reshape(db, QW).astype(MXU_DTYPE)
        x1 = _merge_call(xs, unplace(o8a), ob, unplace(o8c), p["wgate"], p["wpa"], p["wpb"], p["wpc"], p["wout"],
                         p["ln1_g"], p["ln1_b"], tm=tm_s)
        xs = _moe_call(x1, p["wrh"], p["wrl"], p["br"], p["w1"], p["w3"], p["w2"], p["ln2_g"], p["ln2_b"], tm=tm_s)
        outs[3].append(kva.reshape(db, 1, 2, N_KV_HEADS, HEAD_DIM))
        outs[4].append(kvc.reshape(db, 1, 2, N_KV_HEADS, HEAD_DIM))
        outs[5].append(ik_new.reshape(db, 1, IDX_DIM))
        outs[6].append(vv.reshape(db, 1, GMLP_WIDTH))

    return (xp.reshape(bp, tp, d), xs.reshape(db, 1, d)) + tuple(jnp.stack(o) for o in outs)
```

```python
import functools
import math

import jax
import jax.numpy as jnp
from jax import lax
from jax.experimental import pallas as pl
from jax.experimental.pallas import tpu as pltpu

HEAD_DIM = 64
N_HEADS = 8
N_KV_HEADS = 4
GROUP = N_HEADS // N_KV_HEADS
MOBA_BLOCK = 256
MOBA_TOPK = 3
IDX_HEADS = 8
IDX_DIM = 32
DSA_TOPK = 256
GMLP_CHUNK = 128
GMLP_GROUPS = 8
GMLP_WIDTH = 512
BRANCH_WIDTH = 512
N_GROUPS = 4
EXPERTS_PER_GROUP = 4
N_EXPERTS = N_GROUPS * EXPERTS_PER_GROUP
ROPE_THETA = 500000.0
ROPE_FRACTION = 4
LN_EPS = 1e-5
NEG_INF = -1e30
DEEPNORM_ALPHA = (2 * 4) ** 0.25

LANES = 128
PAGE_TOKENS = LANES
KEY_TILE = 256
MXU_DTYPE = jnp.bfloat16
VMEM_LIMIT = 56 * 1024 * 1024

QW = N_HEADS * HEAD_DIM
KW = N_KV_HEADS * HEAD_DIM
IQW = IDX_HEADS * IDX_DIM
INT_MIN = -2147483648
HALF_NEG_KEY = -1892283083

f32 = jnp.float32


def _nt_dot(a, b):
    return lax.dot_general(a, b, (((1,), (1,)), ((), ())), preferred_element_type=f32)


def _dot(a, b):
    return jnp.dot(a, b, preferred_element_type=f32)


def _layer_norm(y, g, b):
    mu = jnp.mean(y, axis=-1, keepdims=True)
    d = y - mu
    var = jnp.mean(d * d, axis=-1, keepdims=True)
    return d * lax.rsqrt(var + LN_EPS) * g + b


def _gelu_tanh(x):
    cdf = 0.5 * (1.0 + jnp.tanh(math.sqrt(2.0 / math.pi) * (x + 0.044715 * (x * x * x))))
    return x * cdf


def _sigmoid(x):
    return 1.0 / (1.0 + jnp.exp(-x))


def _sortable(x):
    bits = pltpu.bitcast(x, jnp.int32)
    return jnp.where(bits < 0, bits ^ jnp.int32(0x7FFFFFFF), bits)


def _const_spec(shape):
    nd = len(shape)
    return pl.BlockSpec(shape, lambda *_: (0,) * nd, pipeline_mode=pl.Buffered(1))


def _rope_chunk(h, c, sa, sb, half):
    return h * c + pltpu.roll(h, half, 1) * sa + pltpu.roll(h, LANES - half, 1) * sb


def _proj_body(x_ref, wm_ref, wg_ref, t64_ref, t32_ref, tl_ref, lng_ref, lnb_ref, ws_ref, bs_ref, *refs, chunked):
    out_refs = refs[3:] if chunked else refs
    qa_ref, qc_ref, kva_ref, kvc_ref, iq_ref, ikw_ref, ob_ref, vv_ref = out_refs[:8]
    if chunked:
        kab_ref, kcb_ref, vta_ref, vtc_ref, ikr_ref, kvat_ref, kvct_ref, ikt_ref = out_refs[8:]
    xb = x_ref[...].astype(MXU_DTYPE)
    tm = xb.shape[0]
    c64, sa64, sb64 = t64_ref[0], t64_ref[1], t64_ref[2]
    c32, sa32, sb32 = t32_ref[0], t32_ref[1], t32_ref[2]
    cl, sal, sbl = tl_ref[0], tl_ref[1], tl_ref[2]
    h64 = (HEAD_DIM // ROPE_FRACTION) // 2
    h32 = (IDX_DIM // ROPE_FRACTION) // 2
    scale = HEAD_DIM ** -0.5 * math.log2(math.e)

    hq = _dot(xb, wm_ref[:, 0:2 * QW])
    for c in range(2 * QW // LANES):
        r = _rope_chunk(hq[:, c * LANES:(c + 1) * LANES], c64, sa64, sb64, h64) * scale
        dst = qa_ref if c < QW // LANES else qc_ref
        cc = c % (QW // LANES)
        dst[:, cc * LANES:(cc + 1) * LANES] = r.astype(dst.dtype)

    o = 2 * QW
    hk = _dot(xb, wm_ref[:, o:o + 2 * KW])
    hv = _dot(xb, wm_ref[:, o + 2 * KW:o + 4 * KW])
    for c in range(2 * KW // LANES):
        r = _rope_chunk(hk[:, c * LANES:(c + 1) * LANES], c64, sa64, sb64, h64)
        v = hv[:, c * LANES:(c + 1) * LANES]
        first = c < KW // LANES
        dst = kva_ref if first else kvc_ref
        cc = c % (KW // LANES)
        dst[:, cc * LANES:(cc + 1) * LANES] = r
        dst[:, KW + cc * LANES:KW + (cc + 1) * LANES] = v
        if chunked:
            kb_ref, vt_ref, kvt_ref = (kab_ref, vta_ref, kvat_ref) if first else (kcb_ref, vtc_ref, kvct_ref)
            kb_ref[:, cc * LANES:(cc + 1) * LANES] = r.astype(kb_ref.dtype)
            for j in range(tm // KEY_TILE):
                vt_ref[j, cc * LANES:(cc + 1) * LANES, :] = v[j * KEY_TILE:(j + 1) * KEY_TILE, :].T.astype(vt_ref.dtype)
            for j in range(tm // PAGE_TOKENS):
                rows = slice(j * PAGE_TOKENS, (j + 1) * PAGE_TOKENS)
                kvt_ref[j, cc * LANES:(cc + 1) * LANES, :] = r[rows, :].T
                kvt_ref[j, KW + cc * LANES:KW + (cc + 1) * LANES, :] = v[rows, :].T

    o = 2 * QW + 4 * KW
    hi = _dot(xb, wm_ref[:, o:o + IQW + LANES])
    for c in range(IQW // LANES):
        r = _rope_chunk(hi[:, c * LANES:(c + 1) * LANES], c32, sa32, sb32, h32)
        iq_ref[:, c * LANES:(c + 1) * LANES] = r.astype(iq_ref.dtype)
    last = _rope_chunk(hi[:, IQW:IQW + LANES], cl, sal, sbl, h32)
    ikw_ref[...] = last
    if chunked:
        lane = lax.broadcasted_iota(jnp.int32, (tm, LANES), 1)
        t = jnp.where(lane < IDX_DIM, last, 0.0)
        t = t + pltpu.roll(t, IDX_DIM, 1)
        t = t + pltpu.roll(t, 2 * IDX_DIM, 1)
        for c in range(IQW // LANES):
            ikr_ref[:, c * LANES:(c + 1) * LANES] = t.astype(ikr_ref.dtype)
        for j in range(tm // PAGE_TOKENS):
            ikt_ref[j] = last[j * PAGE_TOKENS:(j + 1) * PAGE_TOKENS, :].T[0:IDX_DIM, :]

    hg = _dot(xb, wg_ref[...])
    u = _gelu_tanh(hg[:, 0:GMLP_WIDTH])
    vv = _layer_norm(_gelu_tanh(hg[:, GMLP_WIDTH:2 * GMLP_WIDTH]), lng_ref[...], lnb_ref[...])
    vv_ref[...] = vv
    if chunked:
        vb = vv.astype(MXU_DTYPE)
        gd = GMLP_WIDTH // GMLP_GROUPS
        lane_c = lax.broadcasted_iota(jnp.int32, (GMLP_CHUNK, LANES), 1)
        for c in range(tm // GMLP_CHUNK):
            rows = slice(c * GMLP_CHUNK, (c + 1) * GMLP_CHUNK)
            for pr in range(GMLP_WIDTH // LANES):
                vp = vb[rows, pr * LANES:(pr + 1) * LANES]
                g0 = (pr * LANES) // gd
                r0 = _dot(ws_ref[g0], vp)
                r1 = _dot(ws_ref[g0 + 1], vp)
                s = jnp.where(lane_c < gd, r0, r1) + bs_ref[:, pr * LANES:(pr + 1) * LANES]
                ob_ref[rows, pr * LANES:(pr + 1) * LANES] = (u[rows, pr * LANES:(pr + 1) * LANES] * s).astype(ob_ref.dtype)
    else:
        ob_ref[...] = (u * (vv * ws_ref[...] + bs_ref[...])).astype(ob_ref.dtype)


def _proj_call(x, wm, wg, t64, t32, tl, lng, lnb, ws, bs, *, tm, chunked, stacks=(), layer=0):
    n, d = x.shape
    nt = t64.shape[1] // tm
    grid = (n // tm,)
    row = lambda w: pl.BlockSpec((tm, w), lambda i: (i, 0))
    tab = pl.BlockSpec((3, tm, LANES), lambda i: (0, i % nt, 0))
    outs = [
        (QW, MXU_DTYPE), (QW, MXU_DTYPE),
        (2 * KW, f32), (2 * KW, f32),
        (IQW, MXU_DTYPE),
        (LANES, f32),
        (GMLP_WIDTH, MXU_DTYPE),
        (GMLP_WIDTH, f32),
    ]
    out_specs = [row(w) for w, _ in outs]
    out_shape = [jax.ShapeDtypeStruct((n, w), dt) for w, dt in outs]
    if chunked:
        assert tm % KEY_TILE == 0
        vt_spec = pl.BlockSpec((tm // KEY_TILE, KW, KEY_TILE), lambda i: (i, 0, 0))
        vt_shape = jax.ShapeDtypeStruct((n // KEY_TILE, KW, KEY_TILE), MXU_DTYPE)
        out_specs += [row(KW), row(KW), vt_spec, vt_spec, row(IQW)]
        out_shape += [jax.ShapeDtypeStruct((n, KW), MXU_DTYPE)] * 2 + [vt_shape] * 2 \
            + [jax.ShapeDtypeStruct((n, IQW), MXU_DTYPE)]
        assert tm % PAGE_TOKENS == 0 and len(stacks) == 3
        first_stack_out = len(out_specs)
        for s in stacks:
            out_specs.append(pl.BlockSpec((None, tm // PAGE_TOKENS) + s.shape[2:], lambda i: (layer, i, 0, 0)))
            out_shape.append(jax.ShapeDtypeStruct(s.shape, s.dtype))
    inputs = (x, wm, wg, t64, t32, tl, lng, lnb, ws, bs) + tuple(stacks)
    in_specs = [row(d), _const_spec(wm.shape), _const_spec(wg.shape), tab, tab, tab,
                _const_spec(lng.shape), _const_spec(lnb.shape), _const_spec(ws.shape), _const_spec(bs.shape)]
    in_specs += [pl.BlockSpec(memory_space=pl.ANY)] * len(stacks)
    aliases = {len(inputs) - len(stacks) + k: first_stack_out + k for k in range(len(stacks))} if stacks else {}
    return pl.pallas_call(
        functools.partial(_proj_body, chunked=chunked),
        grid=grid,
        in_specs=in_specs,
        out_specs=out_specs,
        out_shape=out_shape,
        input_output_aliases=aliases,
        compiler_params=pltpu.CompilerParams(dimension_semantics=("parallel",), vmem_limit_bytes=VMEM_LIMIT),
        name="proj",
    )(*inputs)


def _head_rows(q_ref, kvh, tq):
    qp = q_ref[:, kvh * LANES:(kvh + 1) * LANES].astype(f32)
    rl = pltpu.roll(qp, HEAD_DIM, 1)
    lo = lax.broadcasted_iota(jnp.int32, (tq, LANES), 1) < HEAD_DIM
    if kvh % 2 == 0:
        g0, g1 = jnp.where(lo, qp, 0.0), jnp.where(lo, rl, 0.0)
    else:
        g0, g1 = jnp.where(lo, 0.0, rl), jnp.where(lo, 0.0, qp)
    return jnp.concatenate([g0, g1], axis=0).astype(MXU_DTYPE)


ACC_ROWS = HEAD_DIM + 16


def _value_rows(vt_ref, kt, kvh):
    vt = vt_ref[kt, kvh * HEAD_DIM:(kvh + 1) * HEAD_DIM, :]
    return jnp.concatenate([vt, jnp.ones((ACC_ROWS - HEAD_DIM, vt.shape[1]), vt.dtype)], axis=0)


def _attend_tile(k_ref, vt_ref, qh, acc_sc, s_sc, state, kt, masks):
    start = pl.multiple_of(kt * KEY_TILE, KEY_TILE)
    for kvh in range(N_KV_HEADS):
        pc = kvh // 2
        s_sc[kvh] = _nt_dot(k_ref[pl.ds(start, KEY_TILE), pc * LANES:(pc + 1) * LANES], qh[kvh])
    out = []
    for kvh in range(N_KV_HEADS):
        m, acc = _softmax_step(s_sc[kvh], masks[kvh], state[kvh], acc_sc[kvh], _value_rows(vt_ref, kt, kvh))
        acc_sc[kvh] = acc
        out.append(m)
    return tuple(out)


def _head_out(acc, tq):
    o = acc[0:HEAD_DIM] / acc[HEAD_DIM:HEAD_DIM + 1]
    return jnp.concatenate([o[:, g * tq:(g + 1) * tq] for g in range(GROUP)], axis=0).T


def _softmax_step(s, mask, m, acc, vt):
    s = jnp.where(mask, s, NEG_INF)
    m_new = jnp.maximum(m, jnp.max(s, axis=0, keepdims=True))
    p = jnp.exp2(s - m_new)
    acc_new = jnp.exp2(m - m_new) * acc + _dot(vt, p.astype(MXU_DTYPE))
    return m_new, acc_new


def _moba_body(q_ref, k_ref, vt_ref, o_ref, kmean_sc, sel_sc, acc_sc, s_sc, *, tq):
    i = pl.program_id(1)
    nb = k_ref.shape[0] // MOBA_BLOCK
    nbp = kmean_sc.shape[0]
    cols = GROUP * tq

    @pl.when(i == 0)
    def _():
        kmean_sc[...] = jnp.zeros_like(kmean_sc)
        for n in range(nb):
            blk = k_ref[n * MOBA_BLOCK:(n + 1) * MOBA_BLOCK, :].astype(f32)
            kmean_sc[n:n + 1, :] = jnp.mean(blk, axis=0, keepdims=True)

    blk_id = lax.broadcasted_iota(jnp.int32, (nbp, cols), 0)
    qh = [_head_rows(q_ref, kvh, tq) for kvh in range(N_KV_HEADS)]

    for kvh in range(N_KV_HEADS):
        pc = kvh // 2
        gate = _nt_dot(kmean_sc[:, pc * LANES:(pc + 1) * LANES].astype(MXU_DTYPE), qh[kvh])
        g = jnp.where(blk_id < i, gate, NEG_INF)
        sel = jnp.zeros((nbp, cols), f32)
        for _ in range(MOBA_TOPK):
            mx = jnp.max(g, axis=0, keepdims=True)
            first = jnp.min(jnp.where(g == mx, blk_id, 1 << 20), axis=0, keepdims=True)
            hit = blk_id == first
            sel = jnp.where(hit & (mx > 0.5 * NEG_INF), 1.0, sel)
            g = jnp.where(hit, -3e38, g)
        sel_sc[kvh] = sel

    acc_sc[...] = jnp.zeros_like(acc_sc)
    key_in_blk = lax.broadcasted_iota(jnp.int32, (KEY_TILE, cols), 0)
    q_in_blk = lax.broadcasted_iota(jnp.int32, (KEY_TILE, cols), 1) % tq
    state = tuple(jnp.full((1, cols), NEG_INF, f32) for _ in range(N_KV_HEADS))
    state = _attend_tile(k_ref, vt_ref, qh, acc_sc, s_sc, state, i, [key_in_blk <= q_in_blk] * N_KV_HEADS)

    def past(n, state):
        chosen = [jnp.broadcast_to(sel_sc[kvh, pl.ds(n, 1), :] > 0.5, (KEY_TILE, cols)) for kvh in range(N_KV_HEADS)]
        return _attend_tile(k_ref, vt_ref, qh, acc_sc, s_sc, state, n, chosen)

    lax.fori_loop(0, i, past, state)
    for kvh in range(N_KV_HEADS):
        o_ref[:, kvh * LANES:(kvh + 1) * LANES] = _head_out(acc_sc[kvh], tq).astype(o_ref.dtype)


def _moba_call(q, kb, vt, *, batch, t):
    tq = MOBA_BLOCK
    assert tq == KEY_TILE and GROUP * HEAD_DIM == LANES
    nq = t // tq
    nbp = -(-nq // 16) * 16
    return pl.pallas_call(
        functools.partial(_moba_body, tq=tq),
        grid=(batch, nq),
        in_specs=[pl.BlockSpec((tq, QW), lambda b, i: (b * nq + i, 0)),
                  pl.BlockSpec((t, KW), lambda b, i: (b, 0)),
                  pl.BlockSpec((nq, KW, KEY_TILE), lambda b, i: (b, 0, 0))],
        out_specs=pl.BlockSpec((tq, QW), lambda b, i: (b * nq + i, 0)),
        out_shape=jax.ShapeDtypeStruct(q.shape, MXU_DTYPE),
        scratch_shapes=[pltpu.VMEM((nbp, KW), f32), pltpu.VMEM((N_KV_HEADS, nbp, GROUP * tq), f32),
                        pltpu.VMEM((N_KV_HEADS, ACC_ROWS, GROUP * tq), f32),
                        pltpu.VMEM((N_KV_HEADS, KEY_TILE, GROUP * tq), f32)],
        compiler_params=pltpu.CompilerParams(dimension_semantics=("parallel", "arbitrary"),
                                             vmem_limit_bytes=VMEM_LIMIT),
        name="moba",
    )(q, kb, vt)


def _select_threshold(count_ge, n_top, rows):
    def step(it, t_u):
        cand_u = t_u | lax.shift_left(jnp.int32(1), jnp.int32(31) - it)
        cnt = count_ge(cand_u ^ jnp.int32(INT_MIN))
        return jnp.where(cnt >= n_top, cand_u, t_u)
    t_u = lax.fori_loop(0, 32, step, jnp.zeros((1, rows), jnp.int32))
    return t_u ^ jnp.int32(INT_MIN)


def _dsa_body(iq_ref, ikw_ref, ikr_ref, q_ref, k_ref, vt_ref, o_ref, key_sc, cut_sc, acc_sc, s_sc,
              *, tq, n_top, idx_bits):
    i = pl.program_id(1)
    ntile = i + 1
    cols = GROUP * tq
    lane_q = lax.broadcasted_iota(jnp.int32, (tq, IQW), 1)
    krow = lax.broadcasted_iota(jnp.int32, (tq, tq), 0)
    qcol = lax.broadcasted_iota(jnp.int32, (tq, tq), 1)
    half_neg = jnp.int32(HALF_NEG_KEY)

    iqv = iq_ref[...]
    iqm = [jnp.where((lane_q >= h * IDX_DIM) & (lane_q < (h + 1) * IDX_DIM), iqv, jnp.zeros_like(iqv))
           for h in range(IDX_HEADS)]
    iwt = ikw_ref[...].T
    iw = [iwt[IDX_DIM + h:IDX_DIM + h + 1, :] for h in range(IDX_HEADS)]
    sc_scale = (IDX_DIM ** -0.5) * (IDX_HEADS ** -0.5)

    def score_tile(kt, _):
        ik_t = ikr_ref[pl.ds(pl.multiple_of(kt * tq, tq), tq), :]
        sc = jnp.zeros((tq, tq), f32)
        for h in range(IDX_HEADS):
            sc = sc + iw[h] * jnp.maximum(_nt_dot(ik_t, iqm[h]), 0.0)
        sc = sc * sc_scale
        sc = jnp.where(kt * tq + krow <= i * tq + qcol, sc, NEG_INF)
        key_sc[kt] = _sortable(sc)
        return 0

    lax.fori_loop(0, ntile, score_tile, 0)

    def count_where(pred):
        nacc = 4

        def body(kt, cs):
            hit = jnp.where(pred(key_sc[kt], kt), 1, 0).reshape(nacc, tq // (8 * nacc), 8, tq)
            return tuple(c + jnp.sum(hit[a], axis=0) for a, c in enumerate(cs))
        cs = lax.fori_loop(0, ntile, body, tuple(jnp.zeros((8, tq), jnp.int32) for _ in range(nacc)))
        return jnp.sum(sum(cs[1:], cs[0]), axis=0, keepdims=True)

    thr = _select_threshold(lambda cand: count_where(lambda k, kt: k >= cand), n_top, tq)
    cnt_gt = count_where(lambda k, kt: k > thr)
    cnt_eq = count_where(lambda k, kt: k == thr)
    need = n_top - cnt_gt

    cut_sc[...] = jnp.full(cut_sc.shape, 1 << 30, jnp.int32)
    excess = (cnt_eq > need) & (thr > half_neg)

    @pl.when(jnp.max(jnp.where(excess, 1, 0)) > 0)
    def _():
        def step(it, t_i):
            cand = t_i | lax.shift_left(jnp.int32(1), jnp.int32(idx_bits - 1) - it)
            cnt = count_where(lambda k, kt: (k == thr) & (kt * tq + krow < cand))
            return jnp.where(cnt < need, cand, t_i)
        t_i = lax.fori_loop(0, idx_bits, step, jnp.zeros((1, tq), jnp.int32))
        cut_sc[...] = jnp.broadcast_to(jnp.where(excess, t_i, 1 << 30), cut_sc.shape)

    cut = cut_sc[0:1, :]

    qh = [_head_rows(q_ref, kvh, tq) for kvh in range(N_KV_HEADS)]
    acc_sc[...] = jnp.zeros_like(acc_sc)

    def attend(kt, state):
        k = key_sc[kt]
        keep = ((k > thr) | ((k == thr) & (kt * tq + krow <= cut))) & (k > half_neg)
        mask = jnp.concatenate([keep] * GROUP, axis=1)
        return _attend_tile(k_ref, vt_ref, qh, acc_sc, s_sc, state, kt, [mask] * N_KV_HEADS)

    lax.fori_loop(0, ntile, attend, tuple(jnp.full((1, cols), NEG_INF, f32) for _ in range(N_KV_HEADS)))
    for kvh in range(N_KV_HEADS):
        o_ref[:, kvh * LANES:(kvh + 1) * LANES] = _head_out(acc_sc[kvh], tq).astype(o_ref.dtype)


def _dsa_call(iq, ikw, ikr, q, kb, vt, *, batch, t):
    tq = KEY_TILE
    nq = t // tq
    n_top = min(DSA_TOPK, t // 4)
    assert n_top <= tq and t % tq == 0 and GROUP * HEAD_DIM == LANES
    rowspec = lambda w: pl.BlockSpec((tq, w), lambda b, i: (b * nq + i, 0))
    seqspec = lambda w: pl.BlockSpec((t, w), lambda b, i: (b, 0))
    return pl.pallas_call(
        functools.partial(_dsa_body, tq=tq, n_top=n_top, idx_bits=max(1, (t - 1).bit_length())),
        grid=(batch, nq),
        in_specs=[rowspec(IQW), rowspec(LANES), seqspec(IQW), rowspec(QW), seqspec(KW),
                  pl.BlockSpec((nq, KW, KEY_TILE), lambda b, i: (b, 0, 0))],
        out_specs=rowspec(QW),
        out_shape=jax.ShapeDtypeStruct(q.shape, MXU_DTYPE),
        scratch_shapes=[pltpu.VMEM((nq, tq, tq), jnp.int32), pltpu.VMEM((8, tq), jnp.int32),
                        pltpu.VMEM((N_KV_HEADS, ACC_ROWS, GROUP * tq), f32),
                        pltpu.VMEM((N_KV_HEADS, KEY_TILE, GROUP * tq), f32)],
        compiler_params=pltpu.CompilerParams(dimension_semantics=("parallel", "arbitrary"),
                                             vmem_limit_bytes=VMEM_LIMIT),
        name="dsa",
    )(iq, ikw, ikr, q, kb, vt)


RADIX_BITS = 4


def _radix_select(count_fn, total_bits, top_bit):
    digit = lax.broadcasted_iota(jnp.int32, (1 << RADIX_BITS, 1), 0)
    t = jnp.zeros((1, 1), jnp.int32)
    for step in range(total_bits // RADIX_BITS):
        shift = top_bit + 1 - RADIX_BITS * (step + 1)
        ok = count_fn(t | lax.shift_left(digit, jnp.int32(shift)))
        best = jnp.sum(jnp.where(ok, 1, 0), axis=0, keepdims=True) - 1
        t = t | lax.shift_left(best, jnp.int32(shift))
    return t


def _decode_seq(abuf, cbuf, ibuf, qa_ref, qc_ref, iq_ref, iw_ref, na_ref, nc_ref, ni_ref, oa_ref, oc_ref,
                *, n_pages, page, n_top):
    past = n_pages * page
    pages_per_block = MOBA_BLOCK // page
    nbp = n_pages // pages_per_block
    lane = lax.broadcasted_iota(jnp.int32, (N_HEADS, LANES), 1)

    def finish(s_list, mask_list, q8, knew, vnew, vbuf, extra_ok):
        s_new = jnp.sum(q8.astype(f32) * knew, axis=1, keepdims=True)
        s_new = jnp.where(extra_ok, s_new, NEG_INF)
        sm = [jnp.where(mk, s, NEG_INF) for s, mk in zip(s_list, mask_list)]
        m = s_new
        for s in sm:
            m = jnp.maximum(m, jnp.max(s, axis=1, keepdims=True))
        p_new = jnp.where(extra_ok, jnp.exp2(s_new - m), 0.0)
        l = p_new
        acc = p_new * vnew
        for p, (s, mk) in enumerate(zip(sm, mask_list)):
            pc = jnp.where(mk, jnp.exp2(s - m), 0.0)
            l = l + jnp.sum(pc, axis=1, keepdims=True)
            acc = acc + _nt_dot(pc.astype(MXU_DTYPE), vbuf[p, KW:2 * KW, :].astype(MXU_DTYPE))
        return acc / l

    q8 = qa_ref[...]
    sa = [_dot(q8, abuf[p, 0:KW, :].astype(MXU_DTYPE)) for p in range(n_pages)]
    gate = jnp.full((N_HEADS, LANES), NEG_INF, f32)
    for n in range(nbp):
        tot = sum(jnp.sum(sa[n * pages_per_block + j], axis=1, keepdims=True) for j in range(pages_per_block))
        gate = jnp.where(lane == n, tot * (1.0 / MOBA_BLOCK), gate)
    sel = jnp.zeros((N_HEADS, LANES), f32)
    g = gate
    for _ in range(min(MOBA_TOPK, nbp + 1)):
        mx = jnp.max(g, axis=1, keepdims=True)
        first = jnp.min(jnp.where(g == mx, lane, 1 << 20), axis=1, keepdims=True)
        hit = lane == first
        sel = jnp.where(hit & (mx > 0.5 * NEG_INF), 1.0, sel)
        g = jnp.where(hit, -3e38, g)
    chosen = [jnp.max(jnp.where(lane == n, sel, 0.0), axis=1, keepdims=True) > 0.5 for n in range(nbp)]
    masks = [jnp.broadcast_to(chosen[p // pages_per_block], sa[p].shape) for p in range(n_pages)]
    oa_ref[...] = finish(sa, masks, q8, na_ref[:, 0:KW], na_ref[:, KW:2 * KW], abuf, True)

    sc_scale = (IDX_DIM ** -0.5) * (IDX_HEADS ** -0.5)
    iq8 = iq_ref[...]
    iw8 = iw_ref[...]
    sc = jnp.concatenate(
        [jnp.sum(iw8 * jnp.maximum(_dot(iq8, ibuf[p].astype(MXU_DTYPE)), 0.0), axis=0, keepdims=True)
         for p in range(n_pages)], axis=1) * sc_scale
    rel_n = jnp.maximum(jnp.sum(iq8.astype(f32) * ni_ref[...], axis=1, keepdims=True), 0.0)
    sc_n = jnp.sum(iw8 * rel_n, axis=0, keepdims=True) * sc_scale
    key = _sortable(sc)
    key_n = _sortable(sc_n)
    idx = lax.broadcasted_iota(jnp.int32, (1, past), 1)
    nd = 1 << RADIX_BITS
    key_b = jnp.broadcast_to(key, (nd, past))
    idx_b = jnp.broadcast_to(idx, (nd, past))

    def count(pred_past, pred_new):
        return jnp.sum(jnp.where(pred_past, 1, 0), axis=1, keepdims=True) + jnp.where(pred_new, 1, 0)

    def enough(cand_u):
        cand = cand_u ^ jnp.int32(INT_MIN)
        return count(key_b >= cand, key_n >= cand) >= n_top

    thr = _radix_select(enough, 32, 31) ^ jnp.int32(INT_MIN)
    need = n_top - count(key > thr, key_n > thr)
    eq = key == thr
    eq_b = jnp.broadcast_to(eq, (nd, past))
    idx_bits = -(-max(1, past.bit_length()) // RADIX_BITS) * RADIX_BITS
    cut = _radix_select(lambda cand: jnp.sum(jnp.where(eq_b & (idx_b < cand), 1, 0), axis=1, keepdims=True) < need,
                        idx_bits, idx_bits - 1)
    eq_past = jnp.sum(jnp.where(eq, 1, 0), axis=1, keepdims=True)
    keep = (key > thr) | (eq & (idx <= cut))
    keep_n = (key_n > thr) | ((key_n == thr) & (eq_past < need))
    q8c = qc_ref[...]
    s_c = [_dot(q8c, cbuf[p, 0:KW, :].astype(MXU_DTYPE)) for p in range(n_pages)]
    masks_c = [jnp.broadcast_to(keep[:, p * page:(p + 1) * page], s_c[p].shape) for p in range(n_pages)]
    oc_ref[...] = finish(s_c, masks_c, q8c, nc_ref[:, 0:KW], nc_ref[:, KW:2 * KW], cbuf, keep_n)


def _decode_body(pt_ref, ca_hbm, cc_hbm, ci_hbm, qa_ref, qc_ref, iq_ref, iw_ref, na_ref, nc_ref, ni_ref,
                 oa_ref, oc_ref, abuf, cbuf, ibuf, sems, *, layer, seqs, n_pages, page, n_top):
    b = pl.program_id(0)
    slot = lax.rem(b, 2)

    def page_copies(step, sl):
        out = []
        for j in range(seqs):
            for p in range(n_pages):
                pg = pt_ref[step * seqs + j, p]
                out.append(pltpu.make_async_copy(ca_hbm.at[layer, pg], abuf.at[sl, j, p], sems.at[0, sl]))
                out.append(pltpu.make_async_copy(cc_hbm.at[layer, pg], cbuf.at[sl, j, p], sems.at[1, sl]))
                out.append(pltpu.make_async_copy(ci_hbm.at[layer, pg], ibuf.at[sl, j, p], sems.at[2, sl]))
        return out

    @pl.when(b == 0)
    def _():
        for c in page_copies(0, 0):
            c.start()

    @pl.when(b + 1 < pl.num_programs(0))
    def _():
        for c in page_copies(b + 1, 1 - slot):
            c.start()

    for c in page_copies(b, slot):
        c.wait()

    for j in range(seqs):
        _decode_seq(abuf.at[slot, j], cbuf.at[slot, j], ibuf.at[slot, j], qa_ref.at[j], qc_ref.at[j], iq_ref.at[j],
                    iw_ref.at[j], na_ref.at[j], nc_ref.at[j], ni_ref.at[j], oa_ref.at[j], oc_ref.at[j],
                    n_pages=n_pages, page=page, n_top=n_top)


def _decode_call(page_table, cache_at, cache_ct, cache_it, layer, q8a, q8c, iq8, iw8, kva, kvc, ik):
    db, n_pages = page_table.shape
    page = cache_at.shape[3]
    past = n_pages * page
    assert past % MOBA_BLOCK == 0 and MOBA_BLOCK % page == 0 and page == LANES
    n_top = min(DSA_TOPK, (past + 1) // 4)
    assert n_top <= past
    seqs = 2 if db % 2 == 0 else 1
    seqspec = lambda r, w: pl.BlockSpec((seqs, r, w), lambda b, pt: (b, 0, 0))
    hbm = pl.BlockSpec(memory_space=pl.ANY)
    grid_spec = pltpu.PrefetchScalarGridSpec(
        num_scalar_prefetch=1,
        grid=(db // seqs,),
        in_specs=[hbm, hbm, hbm,
                  seqspec(N_HEADS, KW), seqspec(N_HEADS, KW), seqspec(IDX_HEADS, IDX_DIM), seqspec(IDX_HEADS, 1),
                  seqspec(1, 2 * KW), seqspec(1, 2 * KW), seqspec(1, IDX_DIM)],
        out_specs=[seqspec(N_HEADS, KW), seqspec(N_HEADS, KW)],
        scratch_shapes=[pltpu.VMEM((2, seqs, n_pages, 2 * KW, page), f32),
                        pltpu.VMEM((2, seqs, n_pages, 2 * KW, page), f32),
                        pltpu.VMEM((2, seqs, n_pages, IDX_DIM, page), f32), pltpu.SemaphoreType.DMA((3, 2))],
    )
    return pl.pallas_call(
        functools.partial(_decode_body, layer=layer, seqs=seqs, n_pages=n_pages, page=page, n_top=n_top),
        grid_spec=grid_spec,
        out_shape=[jax.ShapeDtypeStruct((db, N_HEADS, KW), f32)] * 2,
        compiler_params=pltpu.CompilerParams(dimension_semantics=("arbitrary",), vmem_limit_bytes=VMEM_LIMIT),
        name="decode",
    )(page_table, cache_at, cache_ct, cache_it, q8a, q8c, iq8, iw8, kva, kvc, ik)


def _merge_body(x_ref, oa_ref, ob_ref, oc_ref, wgate_ref, wpa_ref, wpb_ref, wpc_ref, wout_ref, g_ref, b_ref, o_ref):
    x = x_ref[...]
    d = x.shape[1]
    xb = x.astype(MXU_DTYPE)
    merged = None
    for k, (o_r, wp_r) in enumerate(((oa_ref, wpa_ref), (ob_ref, wpb_ref), (oc_ref, wpc_ref))):
        gate = _sigmoid(_dot(xb, wgate_ref[:, k * d:(k + 1) * d]))
        term = gate * _dot(o_r[...].astype(MXU_DTYPE), wp_r[...])
        merged = term if merged is None else merged + term
    y = DEEPNORM_ALPHA * x + _dot(merged.astype(MXU_DTYPE), wout_ref[...])
    o_ref[...] = _layer_norm(y, g_ref[...], b_ref[...])


def _merge_call(x, oa, ob, oc, wgate, wpa, wpb, wpc, wout, g, b, *, tm):
    n, d = x.shape
    row = lambda w: pl.BlockSpec((tm, w), lambda i: (i, 0))
    return pl.pallas_call(
        _merge_body,
        grid=(n // tm,),
        in_specs=[row(d), row(BRANCH_WIDTH), row(BRANCH_WIDTH), row(BRANCH_WIDTH)]
        + [_const_spec(a.shape) for a in (wgate, wpa, wpb, wpc, wout, g, b)],
        out_specs=row(d),
        out_shape=jax.ShapeDtypeStruct((n, d), f32),
        compiler_params=pltpu.CompilerParams(dimension_semantics=("parallel",), vmem_limit_bytes=VMEM_LIMIT),
        name="merge",
    )(x, oa, ob, oc, wgate, wpa, wpb, wpc, wout, g, b)


def _moe_body(x_ref, wrh_ref, wrl_ref, br_ref, w1_ref, w3_ref, w2_ref, g_ref, b_ref, o_ref, hh_sc):
    x = x_ref[...]
    tm = x.shape[0]
    xb = x.astype(MXU_DTYPE)
    xl = (x - xb.astype(f32)).astype(MXU_DTYPE)
    logits = _dot(xb, wrh_ref[...]) + _dot(xb, wrl_ref[...]) + _dot(xl, wrh_ref[...]) + br_ref[...]
    lane = lax.broadcasted_iota(jnp.int32, (tm, LANES), 1)
    big = 1 << 20
    gl = jnp.where(lane < N_GROUPS, logits, -jnp.inf)
    gmax = jnp.max(gl, axis=1, keepdims=True)
    g_sel = jnp.min(jnp.where(gl == gmax, lane, big), axis=1, keepdims=True)
    g_w = 1.0 / jnp.sum(jnp.exp(gl - gmax), axis=1, keepdims=True)
    lo = N_GROUPS + EXPERTS_PER_GROUP * g_sel
    el = jnp.where((lane >= lo) & (lane < lo + EXPERTS_PER_GROUP), logits, -jnp.inf)
    m1 = jnp.max(el, axis=1, keepdims=True)
    i1 = jnp.min(jnp.where(el == m1, lane, big), axis=1, keepdims=True)
    el2 = jnp.where(lane == i1, -jnp.inf, el)
    m2 = jnp.max(el2, axis=1, keepdims=True)
    i2 = jnp.min(jnp.where(el2 == m2, lane, big), axis=1, keepdims=True)
    e2 = jnp.exp(m2 - m1)
    w_first = g_w / (1.0 + e2)
    w_second = g_w * e2 / (1.0 + e2)
    comb = jnp.where(lane == i1, w_first, 0.0) + jnp.where(lane == i2, w_second, 0.0)

    ff = w1_ref.shape[2]
    for e in range(N_EXPERTS):
        hg = _dot(xb, w1_ref[e])
        hu = _dot(xb, w3_ref[e])
        ce = comb[:, N_GROUPS + e:N_GROUPS + e + 1]
        hh_sc[:, e * ff:(e + 1) * ff] = (hg * _sigmoid(hg) * hu * ce).astype(hh_sc.dtype)
    y = DEEPNORM_ALPHA * x + _dot(hh_sc[...], w2_ref[...])
    o_ref[...] = _layer_norm(y, g_ref[...], b_ref[...])


def _moe_call(x, wrh, wrl, br, w1, w3, w2, g, b, *, tm):
    n, d = x.shape
    row = pl.BlockSpec((tm, d), lambda i: (i, 0))
    return pl.pallas_call(
        _moe_body,
        grid=(n // tm,),
        in_specs=[row] + [_const_spec(a.shape) for a in (wrh, wrl, br, w1, w3, w2, g, b)],
        out_specs=row,
        out_shape=jax.ShapeDtypeStruct((n, d), f32),
        scratch_shapes=[pltpu.VMEM((tm, w2.shape[0]), MXU_DTYPE)],
        compiler_params=pltpu.CompilerParams(dimension_semantics=("parallel",), vmem_limit_bytes=VMEM_LIMIT),
        name="moe",
    )(x, wrh, wrl, br, w1, w3, w2, g, b)


def _rope_table(pos, dim, ident_from=None):
    rot = dim // ROPE_FRACTION
    half = rot // 2
    inv_freq = ROPE_THETA ** (-jnp.arange(half, dtype=f32) / half)
    ang = pos.astype(f32)[:, None] * inv_freq[None, :]
    cos, sin = jnp.cos(ang), jnp.sin(ang)
    n = pos.shape[0]
    z = lambda w: jnp.zeros((n, w), f32)
    c = jnp.concatenate([cos, cos, jnp.ones((n, dim - rot), f32)], axis=1)
    sa = jnp.concatenate([z(half), sin, z(dim - rot)], axis=1)
    sb = jnp.concatenate([-sin, z(dim - half)], axis=1)
    tab = jnp.stack([jnp.tile(a, (1, LANES // dim)) for a in (c, sa, sb)])
    if ident_from is not None:
        keep = (jnp.arange(LANES) < ident_from)[None, None, :]
        ident = jnp.stack([jnp.ones((n, LANES), f32), jnp.zeros((n, LANES), f32), jnp.zeros((n, LANES), f32)])
        tab = jnp.where(keep, tab, ident)
    return tab


def _layer_params(l, w_in, w_pa, w_pb, w_pc, w_out, w_s, b_s, ln_v_g, ln_v_b, ln1_g, ln1_b, ln2_g, ln2_b,
                  w_rg, b_rg, w_re, b_re, w_e1, w_e3, w_e2):
    d = w_in.shape[1]
    w = w_in[l]
    offs = {}
    o = 0
    for name, width in (("aq", QW), ("ak", KW), ("av", KW), ("cq", QW), ("ck", KW), ("cv", KW),
                        ("iq", IQW), ("ik", IDX_DIM), ("iw", IDX_HEADS), ("gu", GMLP_WIDTH), ("gv", GMLP_WIDTH),
                        ("ga", d), ("gb", d), ("gc", d)):
        offs[name] = (o, o + width)
        o += width
    assert o == w.shape[1]
    cols = lambda n: w[:, offs[n][0]:offs[n][1]]
    pad = jnp.zeros((d, LANES - IDX_DIM - IDX_HEADS), w.dtype)
    wm = jnp.concatenate([cols("aq"), cols("cq"), cols("ak"), cols("ck"), cols("av"), cols("cv"),
                          cols("iq"), cols("ik"), cols("iw"), pad], axis=1).astype(MXU_DTYPE)
    wg = jnp.concatenate([cols("gu"), cols("gv")], axis=1).astype(MXU_DTYPE)
    wgate = jnp.concatenate([cols("ga"), cols("gb"), cols("gc")], axis=1).astype(MXU_DTYPE)
    causal = jnp.tril(jnp.ones((GMLP_CHUNK, GMLP_CHUNK), dtype=bool))
    gd = GMLP_WIDTH // GMLP_GROUPS
    ws = jnp.where(causal[None], w_s[l], 0)
    wr = jnp.concatenate([w_rg[l], w_re[l], jnp.zeros((d, LANES - N_GROUPS - N_EXPERTS), f32)], axis=1)
    wrh = wr.astype(MXU_DTYPE)
    row = lambda a: a.reshape(1, -1)
    return dict(
        wm=wm, wg=wg, wgate=wgate,
        ws_chunk=ws.astype(MXU_DTYPE),
        bs_chunk=jnp.repeat(b_s[l].T, gd, axis=1),
        ws_row=jnp.repeat(w_s[l][:, 0, 0], gd).reshape(1, -1),
        bs_row=jnp.repeat(b_s[l][:, 0], gd).reshape(1, -1),
        lnv_g=row(ln_v_g[l]), lnv_b=row(ln_v_b[l]),
        wpa=w_pa[l].astype(MXU_DTYPE), wpb=w_pb[l].astype(MXU_DTYPE), wpc=w_pc[l].astype(MXU_DTYPE),
        wout=w_out[l].astype(MXU_DTYPE), ln1_g=row(ln1_g[l]), ln1_b=row(ln1_b[l]),
        wrh=wrh, wrl=(wr - wrh.astype(f32)).astype(MXU_DTYPE),
        br=jnp.concatenate([b_rg[l], b_re[l], jnp.zeros((LANES - N_GROUPS - N_EXPERTS,), f32)]).reshape(1, -1),
        w1=w_e1[l].astype(MXU_DTYPE), w3=w_e3[l].astype(MXU_DTYPE),
        w2=w_e2[l].reshape(-1, d).astype(MXU_DTYPE), ln2_g=row(ln2_g[l]), ln2_b=row(ln2_b[l]),
    )


def kernel(x_prompt, x_sample, cache_a_kv, cache_c_kv, cache_c_kidx, page_table, w_in, w_pa, w_pb, w_pc, w_out, w_s, b_s, ln_v_g, ln_v_b, ln1_g, ln1_b, ln2_g, ln2_b, w_rg, b_rg, w_re, b_re, w_e1, w_e3, w_e2):
    bp, tp, d = x_prompt.shape
    db, ts, _ = x_sample.shape
    depth = w_in.shape[0]
    n_phys, page = cache_a_kv.shape[1], cache_a_kv.shape[2]
    n_pages = page_table.shape[1]
    past = n_pages * page
    assert ts == 1 and tp % MOBA_BLOCK == 0 and cache_a_kv.shape[3:] == (2, N_KV_HEADS, HEAD_DIM)

    cache_a = jnp.transpose(cache_a_kv, (0, 1, 3, 4, 5, 2)).reshape(depth, n_phys, 2 * KW, page)
    cache_c = jnp.transpose(cache_c_kv, (0, 1, 3, 4, 5, 2)).reshape(depth, n_phys, 2 * KW, page)
    cache_i = jnp.transpose(cache_c_kidx, (0, 1, 3, 2))
    page_table = page_table.astype(jnp.int32)

    tm_p = 512 if tp % 512 == 0 else MOBA_BLOCK
    tm_s = db
    pos_p = jnp.arange(tp, dtype=jnp.int32)
    pos_s = jnp.full((db,), past, jnp.int32)
    tabs_p = (_rope_table(pos_p, HEAD_DIM), _rope_table(pos_p, IDX_DIM), _rope_table(pos_p, IDX_DIM, IDX_DIM))
    tabs_s = (_rope_table(pos_s, HEAD_DIM), _rope_table(pos_s, IDX_DIM), _rope_table(pos_s, IDX_DIM, IDX_DIM))

    xp = x_prompt.reshape(bp * tp, d)
    xs = x_sample.reshape(db, d)
    outs = [[] for _ in range(4)]
    eye = jnp.eye(N_KV_HEADS, dtype=f32)
    assert page == PAGE_TOKENS
    n_pg = bp * tp // page
    stacks = [jnp.zeros((depth, n_pg, 2 * KW, page), f32), jnp.zeros((depth, n_pg, 2 * KW, page), f32),
              jnp.zeros((depth, n_pg, IDX_DIM, page), f32)]
    for l in range(depth):
        p = _layer_params(l, w_in, w_pa, w_pb, w_pc, w_out, w_s, b_s, ln_v_g, ln_v_b, ln1_g, ln1_b, ln2_g, ln2_b,
                          w_rg, b_rg, w_re, b_re, w_e1, w_e3, w_e2)

        qa, qc, _, _, iq, ikw, ob, _, kab, kcb, vta, vtc, ikr, *stacks = _proj_call(
            xp, p["wm"], p["wg"], *tabs_p, p["lnv_g"], p["lnv_b"], p["ws_chunk"], p["bs_chunk"],
            tm=tm_p, chunked=True, stacks=tuple(stacks), layer=l)
        oa = _moba_call(qa, kab, vta, batch=bp, t=tp)
        oc = _dsa_call(iq, ikw, ikr, qc, kcb, vtc, batch=bp, t=tp)
        x1 = _merge_call(xp, oa, ob, oc, p["wgate"], p["wpa"], p["wpb"], p["wpc"], p["wout"],
                         p["ln1_g"], p["ln1_b"], tm=tm_p)
        xp = _moe_call(x1, p["wrh"], p["wrl"], p["br"], p["w1"], p["w3"], p["w2"], p["ln2_g"], p["ln2_b"], tm=tm_p)

        qa, qc, kva, kvc, iq, ikw, ob, vv = _proj_call(
            xs, p["wm"], p["wg"], *tabs_s, p["lnv_g"], p["lnv_b"], p["ws_row"], p["bs_row"],
            tm=tm_s, chunked=False)
        place = lambda q: jnp.einsum("bkgd,kj->bkgjd", q.astype(f32).reshape(db, N_KV_HEADS, GROUP, HEAD_DIM),
                                     eye).reshape(db, N_HEADS, KW).astype(MXU_DTYPE)
        ik_new = ikw[:, 0:IDX_DIM]
        o8a, o8c = _decode_call(
            page_table, cache_a, cache_c, cache_i, l, place(qa), place(qc),
            iq.reshape(db, IDX_HEADS, IDX_DIM), ikw[:, IDX_DIM:IDX_DIM + IDX_HEADS].reshape(db, IDX_HEADS, 1),
            kva.reshape(db, 1, 2 * KW), kvc.reshape(db, 1, 2 * KW), ik_new.reshape(db, 1, IDX_DIM))
        unplace = lambda o: jnp.einsum("bkgkd->bkgd", o.reshape(db, N_KV_HEADS, GROUP, N_KV_HEADS, HEAD_DIM)
                                       ).reshape(db, QW).astype(MXU_DTYPE)
        x1 = _merge_call(xs, unplace(o8a), ob, unplace(o8c), p["wgate"], p["wpa"], p["wpb"], p["wpc"], p["wout"],
                         p["ln1_g"], p["ln1_b"], tm=tm_s)
        xs = _moe_call(x1, p["wrh"], p["wrl"], p["br"], p["w1"], p["w3"], p["w2"], p["ln2_g"], p["ln2_b"], tm=tm_s)
        outs[0].append(kva.reshape(db, 1, 2, N_KV_HEADS, HEAD_DIM))
        outs[1].append(kvc.reshape(db, 1, 2, N_KV_HEADS, HEAD_DIM))
        outs[2].append(ik_new.reshape(db, 1, IDX_DIM))
        outs[3].append(vv.reshape(db, 1, GMLP_WIDTH))

    kv_leaf = lambda s: jnp.transpose(s.reshape(depth, bp, tp // page, 2, N_KV_HEADS, HEAD_DIM, page),
                                      (0, 1, 2, 6, 3, 4, 5))
    ik_leaf = jnp.transpose(stacks[2].reshape(depth, bp, tp // page, IDX_DIM, page), (0, 1, 2, 4, 3))
    return ((xp.reshape(bp, tp, d), xs.reshape(db, 1, d), kv_leaf(stacks[0]), kv_leaf(stacks[1]), ik_leaf)
            + tuple(jnp.stack(o) for o in outs))
```
